```python
import math, functools
import jax, jax.numpy as jnp
from jax import lax
import numpy as np

D_MODEL = 1024
BATCH = 8
SEQ = 2048
DEPTH = 2
DEC_BATCH = 128
DEC_SEQ = 8
PAST_LEN = 2048
PAGE_SIZE = 128

N_META = 16
N_HEADS = D_MODEL // 256
HEAD_DIM = 64
ATT_WIDTH = N_HEADS * 2 * HEAD_DIM
SSM_WIDTH = D_MODEL // 2
GROUP_CH = 16
N_GROUPS = SSM_WIDTH // GROUP_CH
STATE_DIM = 64
D_FF = 4 * D_MODEL
IN_WIDTH = 3 * ATT_WIDTH + SSM_WIDTH + 2 * D_MODEL
N_BUCKETS = 32
MAX_DISTANCE = 128
Q_BLOCK = 128
ALPHA = (2.0 * DEPTH) ** 0.25
BETA = (8.0 * DEPTH) ** -0.25
EPS = 1e-5

kernel_name = 'diffattn_s5_gated_hybrid_step'


def layer_norm(x, g, b):
    xf = x.astype(jnp.float32)
    mu = jnp.mean(xf, -1, keepdims=True)
    var = jnp.mean(jnp.square(xf - mu), -1, keepdims=True)
    y = (xf - mu) * lax.rsqrt(var + EPS) * g.astype(jnp.float32) + b.astype(jnp.float32)
    return y.astype(x.dtype)


def t5_bucket(rel):
    n = jnp.maximum(rel, 0)
    exact = N_BUCKETS // 2
    large = exact + (jnp.log(jnp.maximum(n, 1).astype(jnp.float32) / exact)
                     / math.log(MAX_DISTANCE / exact) * (N_BUCKETS - exact)).astype(jnp.int32)
    return jnp.where(n < exact, n, jnp.minimum(large, N_BUCKETS - 1))


def diff_attend(q, k, v, q_pos, k_pos, lam, rel_bias):
    s = jnp.einsum('bqhcd,bkhcd->bhcqk', q, k).astype(jnp.float32) * (HEAD_DIM ** -0.5)
    bias = jnp.transpose(rel_bias[t5_bucket(q_pos[:, None] - k_pos[None, :])], (2, 0, 1))
    mask = k_pos[None, :] <= q_pos[:, None]
    s = jnp.where(mask, s + bias.astype(jnp.float32)[None, :, None], -jnp.inf)
    p = jax.nn.softmax(s, axis=-1)
    w = p[:, :, 0] - lam * p[:, :, 1]
    return jnp.einsum('bhqk,bkhe->bqhe', w.astype(v.dtype), v)


def attend_prompt(q, k, v, lam, rel_bias):
    b, t = q.shape[:2]
    nb = -(-t // Q_BLOCK)
    tp = nb * Q_BLOCK
    pad = tp - t
    qp = jnp.pad(q, ((0, 0), (0, pad), (0, 0), (0, 0), (0, 0)))
    kp = jnp.pad(k, ((0, 0), (0, pad), (0, 0), (0, 0), (0, 0)))
    vp = jnp.pad(v, ((0, 0), (0, pad), (0, 0), (0, 0)))
    k_pos = jnp.arange(tp, dtype=jnp.int32)
    q_blocks = jnp.moveaxis(qp.reshape(b, nb, Q_BLOCK, N_HEADS, 2, HEAD_DIM), 1, 0)

    def one_block(args):
        qb, i = args
        q_pos = i * Q_BLOCK + jnp.arange(Q_BLOCK, dtype=jnp.int32)
        return diff_attend(qb, kp, vp, q_pos, k_pos, lam, rel_bias)

    o = lax.map(one_block, (q_blocks, jnp.arange(nb, dtype=jnp.int32)))
    return jnp.moveaxis(o, 0, 1).reshape(b, tp, N_HEADS, 2 * HEAD_DIM)[:, :t]


def attend_sample(q, k, v, lam, rel_bias, k_cache, v_cache, page_table):
    b, s = q.shape[:2]
    past = page_table.shape[1] * PAGE_SIZE
    k_past = k_cache[page_table].reshape(b, past, N_HEADS, 2, HEAD_DIM).astype(k.dtype)
    v_past = v_cache[page_table].reshape(b, past, N_HEADS, 2 * HEAD_DIM).astype(v.dtype)
    k_all = jnp.concatenate([k_past, k], axis=1)
    v_all = jnp.concatenate([v_past, v], axis=1)
    q_pos = past + jnp.arange(s, dtype=jnp.int32)
    k_pos = jnp.arange(past + s, dtype=jnp.int32)
    return diff_attend(q, k_all, v_all, q_pos, k_pos, lam, rel_bias)


def _complex_affine_combine(e1, e2):
    ar1, ai1, br1, bi1 = e1
    ar2, ai2, br2, bi2 = e2
    return (ar2 * ar1 - ai2 * ai1, ar2 * ai1 + ai2 * ar1,
            ar2 * br1 - ai2 * bi1 + br2, ar2 * bi1 + ai2 * br1 + bi2)


def s5_scan(u, h0_re, h0_im, a_re, a_im, log_dt, b_re, b_im, c_re, c_im, d):
    f32 = jnp.float32
    bsz, t, _ = u.shape
    uf = u.astype(f32).reshape(bsz, t, N_GROUPS, GROUP_CH)
    a_re = a_re.astype(f32)
    a_im = a_im.astype(f32)
    dt = jnp.exp(log_dt.astype(f32))[:, None]
    mag = jnp.exp(a_re * dt)
    lb_re = mag * jnp.cos(a_im * dt)
    lb_im = mag * jnp.sin(a_im * dt)
    nr, ni = lb_re - 1.0, lb_im
    den = a_re * a_re + a_im * a_im
    f_re = (nr * a_re + ni * a_im) / den
    f_im = (ni * a_re - nr * a_im) / den
    b_re = b_re.astype(f32)
    b_im = b_im.astype(f32)
    bb_re = f_re[..., None] * b_re - f_im[..., None] * b_im
    bb_im = f_re[..., None] * b_im + f_im[..., None] * b_re
    x_re = jnp.einsum('gpc,btgc->btgp', bb_re, uf)
    x_im = jnp.einsum('gpc,btgc->btgp', bb_im, uf)
    h0_re = h0_re.astype(f32)
    h0_im = h0_im.astype(f32)
    x_re = x_re.at[:, 0].add(lb_re * h0_re - lb_im * h0_im)
    x_im = x_im.at[:, 0].add(lb_re * h0_im + lb_im * h0_re)
    ar = jnp.broadcast_to(lb_re, x_re.shape)
    ai = jnp.broadcast_to(lb_im, x_im.shape)
    _, _, h_re, h_im = lax.associative_scan(_complex_affine_combine, (ar, ai, x_re, x_im), axis=1)
    y = (jnp.einsum('gcp,btgp->btgc', c_re.astype(f32), h_re)
         - jnp.einsum('gcp,btgp->btgc', c_im.astype(f32), h_im)
         + d.astype(f32) * uf)
    return y.reshape(bsz, t, SSM_WIDTH), h_re[:, -1], h_im[:, -1]


def layer(x, l, attend, h0_re, h0_im, w_in, lq1, lk1, lq2, lk2, subln_g, w_att_up,
          a_re, a_im, log_dt, b_re, b_im, c_re, c_im, d, w_glu, w_out,
          ln1_g, ln1_b, w_up, w_down, ln2_g, ln2_b):
    f32 = jnp.float32
    bsz, t, _ = x.shape
    proj = x @ w_in
    q, k, v, u, ga, gb = jnp.split(
        proj, [ATT_WIDTH, 2 * ATT_WIDTH, 3 * ATT_WIDTH, 3 * ATT_WIDTH + SSM_WIDTH,
               3 * ATT_WIDTH + SSM_WIDTH + D_MODEL], axis=-1)
    q = q.reshape(bsz, t, N_HEADS, 2, HEAD_DIM)
    k = k.reshape(bsz, t, N_HEADS, 2, HEAD_DIM)
    v = v.reshape(bsz, t, N_HEADS, 2 * HEAD_DIM)
    lam_init = 0.8 - 0.6 * math.exp(-0.3 * l)
    lam = (jnp.exp(jnp.sum(lq1.astype(f32) * lk1.astype(f32)))
           - jnp.exp(jnp.sum(lq2.astype(f32) * lk2.astype(f32))) + lam_init)
    o = attend(q, k, v, lam).astype(f32)
    o = (o * lax.rsqrt(jnp.mean(jnp.square(o), -1, keepdims=True) + EPS)
         * subln_g.astype(f32) * (1.0 - lam_init))
    a_branch = o.reshape(bsz, t, ATT_WIDTH).astype(x.dtype) @ w_att_up
    y, h_re, h_im = s5_scan(u, h0_re, h0_im, a_re, a_im, log_dt, b_re, b_im, c_re, c_im, d)
    z = jax.nn.gelu(y.astype(x.dtype))
    glu = z @ w_glu
    b_branch = glu[..., :D_MODEL] * jax.nn.sigmoid(glu[..., D_MODEL:])
    mix = (jax.nn.sigmoid(ga) * a_branch + jax.nn.sigmoid(gb) * b_branch) @ w_out
    x = layer_norm(ALPHA * x + mix, ln1_g, ln1_b)
    ffn = jnp.square(jax.nn.relu(x @ w_up)) @ w_down
    x = layer_norm(ALPHA * x + ffn, ln2_g, ln2_b)
    return x, k, v, h_re, h_im


def setup_inputs(seed: int = 0) -> dict:
    key = jax.random.key(seed)
    ks = iter(jax.random.split(key, 48))
    f32 = jnp.float32

    def nrm(shape, scale):
        return jax.random.normal(next(ks), shape, f32) * scale

    n_pages = PAST_LEN // PAGE_SIZE
    n_used = DEC_BATCH * n_pages
    n_pool = (n_used * 5) // 4
    perm = jax.random.permutation(next(ks), n_pool)
    page_table = perm[:n_used].reshape(DEC_BATCH, n_pages).astype(jnp.int32)
    n_idx = jnp.arange(STATE_DIM, dtype=f32)
    return {
        'x_prompt': nrm((BATCH, SEQ, D_MODEL), 1.0),
        'x_sample': nrm((DEC_BATCH, DEC_SEQ, D_MODEL), 1.0),
        'cache_k': nrm((DEPTH, n_pool, PAGE_SIZE, N_HEADS, 2, HEAD_DIM), 1.0),
        'cache_v': nrm((DEPTH, n_pool, PAGE_SIZE, N_HEADS, 2 * HEAD_DIM), 1.0),
        'state_ssm_re': nrm((DEPTH, DEC_BATCH, N_GROUPS, STATE_DIM), 0.1),
        'state_ssm_im': nrm((DEPTH, DEC_BATCH, N_GROUPS, STATE_DIM), 0.1),
        'page_table': page_table,
        'meta_tokens': nrm((N_META, D_MODEL), 1.0),
        'ln_in_g': 1.0 + nrm((D_MODEL,), 0.02),
        'ln_in_b': nrm((D_MODEL,), 0.02),
        'rel_bias': nrm((N_BUCKETS, N_HEADS), 0.5),
        'w_in': nrm((DEPTH, D_MODEL, IN_WIDTH), D_MODEL ** -0.5),
        'lambda_q1': nrm((DEPTH, HEAD_DIM), 0.1),
        'lambda_k1': nrm((DEPTH, HEAD_DIM), 0.1),
        'lambda_q2': nrm((DEPTH, HEAD_DIM), 0.1),
        'lambda_k2': nrm((DEPTH, HEAD_DIM), 0.1),
        'subln_g': 1.0 + nrm((DEPTH, 2 * HEAD_DIM), 0.02),
        'w_att_up': nrm((DEPTH, ATT_WIDTH, D_MODEL), ATT_WIDTH ** -0.5),
        'ssm_a_re': -0.5 + nrm((DEPTH, N_GROUPS, STATE_DIM), 0.01),
        'ssm_a_im': math.pi * n_idx + nrm((DEPTH, N_GROUPS, STATE_DIM), 0.01),
        'ssm_log_dt': jax.random.uniform(next(ks), (DEPTH, N_GROUPS), f32,
                                         math.log(1e-3), math.log(1e-1)),
        'ssm_b_re': nrm((DEPTH, N_GROUPS, STATE_DIM, GROUP_CH), (2 * GROUP_CH) ** -0.5),
        'ssm_b_im': nrm((DEPTH, N_GROUPS, STATE_DIM, GROUP_CH), (2 * GROUP_CH) ** -0.5),
        'ssm_c_re': nrm((DEPTH, N_GROUPS, GROUP_CH, STATE_DIM), STATE_DIM ** -0.5),
        'ssm_c_im': nrm((DEPTH, N_GROUPS, GROUP_CH, STATE_DIM), STATE_DIM ** -0.5),
        'ssm_d': nrm((DEPTH, N_GROUPS, GROUP_CH), 1.0),
        'w_glu': nrm((DEPTH, SSM_WIDTH, 2 * D_MODEL), SSM_WIDTH ** -0.5),
        'w_out': nrm((DEPTH, D_MODEL, D_MODEL), D_MODEL ** -0.5 * BETA),
        'ln1_g': 1.0 + nrm((DEPTH, D_MODEL), 0.02),
        'ln1_b': nrm((DEPTH, D_MODEL), 0.02),
        'w_up': nrm((DEPTH, D_MODEL, D_FF), D_MODEL ** -0.5),
        'w_down': nrm((DEPTH, D_FF, D_MODEL), D_FF ** -0.5 * BETA),
        'ln2_g': 1.0 + nrm((DEPTH, D_MODEL), 0.02),
        'ln2_b': nrm((DEPTH, D_MODEL), 0.02),
    }


def reference(x_prompt, x_sample, cache_k, cache_v, state_ssm_re, state_ssm_im, page_table,
              meta_tokens, ln_in_g, ln_in_b, rel_bias, w_in, lambda_q1, lambda_k1,
              lambda_q2, lambda_k2, subln_g, w_att_up, ssm_a_re, ssm_a_im, ssm_log_dt,
              ssm_b_re, ssm_b_im, ssm_c_re, ssm_c_im, ssm_d, w_glu, w_out, ln1_g, ln1_b,
              w_up, w_down, ln2_g, ln2_b):
    bsz = x_prompt.shape[0]
    meta = jnp.broadcast_to(meta_tokens[None].astype(x_prompt.dtype), (bsz, N_META, D_MODEL))
    xp = layer_norm(jnp.concatenate([meta, x_prompt], axis=1), ln_in_g, ln_in_b)
    xs = layer_norm(x_sample, ln_in_g, ln_in_b)
    h0p_re = jnp.zeros((bsz, N_GROUPS, STATE_DIM), jnp.float32)
    h0p_im = jnp.zeros((bsz, N_GROUPS, STATE_DIM), jnp.float32)
    kp_l, vp_l, hpr_l, hpi_l = [], [], [], []
    ks_l, vs_l, hsr_l, hsi_l = [], [], [], []
    for l in range(DEPTH):
        lw = (w_in[l], lambda_q1[l], lambda_k1[l], lambda_q2[l], lambda_k2[l], subln_g[l],
              w_att_up[l], ssm_a_re[l], ssm_a_im[l], ssm_log_dt[l], ssm_b_re[l], ssm_b_im[l],
              ssm_c_re[l], ssm_c_im[l], ssm_d[l], w_glu[l], w_out[l], ln1_g[l], ln1_b[l],
              w_up[l], w_down[l], ln2_g[l], ln2_b[l])
        att_p = functools.partial(attend_prompt, rel_bias=rel_bias)
        xp, kp, vp, hpr, hpi = layer(xp, l, att_p, h0p_re, h0p_im, *lw)
        att_s = functools.partial(attend_sample, rel_bias=rel_bias, k_cache=cache_k[l],
                                  v_cache=cache_v[l], page_table=page_table)
        xs, ksn, vsn, hsr, hsi = layer(xs, l, att_s, state_ssm_re[l], state_ssm_im[l], *lw)
        kp_l.append(kp)
        vp_l.append(vp)
        hpr_l.append(hpr)
        hpi_l.append(hpi)
        ks_l.append(ksn)
        vs_l.append(vsn)
        hsr_l.append(hsr)
        hsi_l.append(hsi)
    y_prompt = xp[:, N_META:]
    y_sample = xs
    return (y_prompt, y_sample, jnp.stack(kp_l), jnp.stack(vp_l), jnp.stack(hpr_l),
            jnp.stack(hpi_l), jnp.stack(ks_l), jnp.stack(vs_l), jnp.stack(hsr_l),
            jnp.stack(hsi_l))
```

```python
import functools
import math

import jax
import jax.numpy as jnp
from jax import lax
from jax.experimental import pallas as pl
from jax.experimental.pallas import tpu as pltpu

F32 = jnp.float32
BF16 = jnp.bfloat16

N_META = 16
HEAD_DIM = 64
HEAD_WIDTH = 2 * HEAD_DIM
GROUP_CH = 16
STATE_DIM = 64
PAGE_SIZE = 128
LANES = 128
N_BUCKETS = 32
MAX_DISTANCE = 128
EPS = 1e-5

ATT_Q_TILE = 256
SSM_CHUNK = 48
V7X_VMEM_LIMIT = 56 * 1024 * 1024


def _layer_norm(x, g, b):
    mu = jnp.mean(x, -1, keepdims=True)
    xc = x - mu
    var = jnp.mean(xc * xc, -1, keepdims=True)
    return xc * lax.rsqrt(var + EPS) * g + b


def _dot(a, b):
    return jnp.dot(a, b, preferred_element_type=F32)


def _dot_nt(a, b):
    return lax.dot_general(a, b, (((1,), (1,)), ((), ())), preferred_element_type=F32)


def _resident(shape):
    return pl.BlockSpec(shape, lambda *_: (0,) * len(shape), pipeline_mode=pl.Buffered(1))


def _row_tile(n_rows, target):
    best = None
    for t in range(16, min(n_rows, target) + 1, 16):
        if n_rows % t == 0:
            best = t
    assert best is not None, n_rows
    return best


def _in_proj_kernel(x_ref, g_ref, b_ref, w_ref, xo_ref, q_ref, kf_ref, kb_ref, vf_ref, vb_ref,
                    u_ref, ga_ref, gb_ref, *, apply_ln, att_w, ssm_w, d_model):
    x = x_ref[...]
    if apply_ln:
        x = _layer_norm(x, g_ref[...], b_ref[...])
    xo_ref[...] = x
    xb = x.astype(BF16)
    c0 = 0

    def mm(width):
        nonlocal c0
        r = _dot(xb, w_ref[:, c0:c0 + width])
        c0 += width
        return r

    q_ref[...] = (mm(att_w) * (HEAD_DIM ** -0.5)).astype(BF16)
    k = mm(att_w)
    kf_ref[...] = k
    kb_ref[...] = k.astype(BF16)
    v = mm(att_w)
    vf_ref[...] = v
    vb_ref[...] = v.astype(BF16)
    u_ref[...] = mm(ssm_w)
    ga_ref[...] = jax.nn.sigmoid(mm(d_model))
    gb_ref[...] = jax.nn.sigmoid(mm(d_model))


def _in_proj(x, ln_g, ln_b, w_in_bf, *, apply_ln, tm, att_w, ssm_w):
    n, d = x.shape
    in_w = w_in_bf.shape[1]
    row = lambda w: pl.BlockSpec((tm, w), lambda i: (i, 0))
    outs = [(d, F32), (att_w, BF16), (att_w, F32), (att_w, BF16), (att_w, F32), (att_w, BF16),
            (ssm_w, F32), (d, F32), (d, F32)]
    return pl.pallas_call(
        functools.partial(_in_proj_kernel, apply_ln=apply_ln, att_w=att_w, ssm_w=ssm_w, d_model=d),
        grid=(n // tm,),
        in_specs=[row(d), _resident((1, d)), _resident((1, d)), _resident((d, in_w))],
        out_specs=[row(w) for w, _ in outs],
        out_shape=[jax.ShapeDtypeStruct((n, w), dt) for w, dt in outs],
        compiler_params=pltpu.CompilerParams(dimension_semantics=("parallel",),
                                             vmem_limit_bytes=V7X_VMEM_LIMIT),
        name="in_proj",
    )(x, ln_g, ln_b, w_in_bf)


def _softmax_block(state, s, v_blk):
    m_blk = jnp.max(s, -1, keepdims=True)
    if state is None:
        m_new = m_blk
        p = jnp.exp(s - m_new)
        return m_new, jnp.sum(p, -1, keepdims=True), _dot(p.astype(BF16), v_blk)
    m, l, acc = state
    m_new = jnp.maximum(m, m_blk)
    alpha = jnp.exp(m - m_new)
    p = jnp.exp(s - m_new)
    return (m_new, alpha * l + jnp.sum(p, -1, keepdims=True),
            alpha * acc + _dot(p.astype(BF16), v_blk))


def _attn_finish(st1, st2, lam, g, sub_scale):
    o = st1[2] / st1[1] - lam * (st2[2] / st2[1])
    o = o * lax.rsqrt(jnp.mean(o * o, -1, keepdims=True) + EPS) * g * sub_scale
    return o.astype(BF16)


def _attn_prompt_kernel(lam_ref, bfar_ref, q_ref, k_ref, v_ref, d0_ref, d1_ref, mb_ref, mm_ref,
                        g_ref, o_ref, k1_ref, k2_ref, *, tq, n_q, sub_scale):
    h = pl.program_id(1)
    lam = lam_ref[0]
    b_far = bfar_ref[h]
    g = g_ref[...]
    k = k_ref[0]
    lane = lax.broadcasted_iota(jnp.int32, k.shape, 1)
    zero = jnp.zeros_like(k)
    k1_ref[...] = jnp.where(lane < HEAD_DIM, k, zero)
    k2_ref[...] = jnp.where(lane >= HEAD_DIM, k, zero)

    def block(states, q_t, k0, size, bias):
        v_blk = v_ref[0, pl.ds(k0, size), :]
        out = []
        for kc_ref, st in zip((k1_ref, k2_ref), states):
            s = _dot_nt(q_t, kc_ref[pl.ds(k0, size), :]) + bias
            out.append(_softmax_block(st, s, v_blk))
        return tuple(out)

    q_m = q_ref[0, 0:N_META, :]
    st = block((None, None), q_m, 0, mm_ref.shape[2], mm_ref[0])
    o_ref[0, 0:N_META, :] = _attn_finish(st[0], st[1], lam, g, sub_scale)

    q_t = q_ref[0, N_META:N_META + tq, :]
    st = block((None, None), q_t, N_META, tq, d0_ref[0])
    st = block(st, q_t, 0, N_META, mb_ref[0])
    o_ref[0, N_META:N_META + tq, :] = _attn_finish(st[0], st[1], lam, g, sub_scale)

    def q_tile(i, carry):
        r0 = pl.multiple_of(N_META + i * tq, 16)
        q_t = q_ref[0, pl.ds(r0, tq), :]
        st = block((None, None), q_t, r0, tq, d0_ref[0])
        st = block(st, q_t, pl.multiple_of(r0 - tq, 16), tq, d1_ref[0])
        st = block(st, q_t, 0, N_META, b_far)

        def far(j, st):
            return block(st, q_t, pl.multiple_of(N_META + j * tq, 16), tq, b_far)

        st = lax.fori_loop(0, i - 1, far, st)
        o_ref[0, pl.ds(r0, tq), :] = _attn_finish(st[0], st[1], lam, g, sub_scale)
        return carry

    lax.fori_loop(1, n_q, q_tile, 0)


def _attn_prompt(lam, b_far, q, kb, vb, d0, d1, mb, mm, g, *, sub_scale):
    bsz, t, att_w = q.shape
    n_heads = att_w // HEAD_WIDTH
    tq = d0.shape[1]
    n_q = (t - N_META) // tq
    assert n_q * tq + N_META == t
    smem = pl.BlockSpec(memory_space=pltpu.SMEM)
    seq = pl.BlockSpec((1, t, HEAD_WIDTH), lambda b, h: (b, 0, h))
    per_head = lambda a: pl.BlockSpec((1,) + a.shape[1:], lambda b, h: (h, 0, 0))
    return pl.pallas_call(
        functools.partial(_attn_prompt_kernel, tq=tq, n_q=n_q, sub_scale=sub_scale),
        grid=(bsz, n_heads),
        in_specs=[smem, smem, seq, seq, seq, per_head(d0), per_head(d1), per_head(mb), per_head(mm),
                  pl.BlockSpec((1, HEAD_WIDTH), lambda b, h: (0, 0))],
        out_specs=seq,
        out_shape=jax.ShapeDtypeStruct(q.shape, BF16),
        scratch_shapes=[pltpu.VMEM((t, HEAD_WIDTH), BF16), pltpu.VMEM((t, HEAD_WIDTH), BF16)],
        compiler_params=pltpu.CompilerParams(dimension_semantics=("parallel", "parallel"),
                                             vmem_limit_bytes=V7X_VMEM_LIMIT),
        name="attn_prompt",
    )(lam, b_far, q, kb, vb, d0, d1, mb, mm, g)


def _attn_sample_kernel(pt_ref, lam_ref, q_ref, kn_ref, vn_ref, bias_ref, g_ref, *rest,
                        n_pages, n_heads, dec_seq, sub_scale):
    k_pages = rest[:n_pages]
    v_pages = rest[n_pages:2 * n_pages]
    o_ref, s_ref = rest[2 * n_pages:]
    del pt_ref
    lam = lam_ref[0]
    att_w = q_ref.shape[2]
    n_rows = 2 * n_heads * dec_seq
    q = q_ref[0]
    q_rep = jnp.concatenate([q] * (2 * n_heads), axis=0)
    row_blk = lax.broadcasted_iota(jnp.int32, (n_rows, att_w), 0) // dec_seq
    col_blk = lax.broadcasted_iota(jnp.int32, (n_rows, att_w), 1) // HEAD_DIM
    q_exp = jnp.where(row_blk == col_blk, q_rep, 0.0).astype(BF16)

    pad = jnp.zeros((PAGE_SIZE - dec_seq, att_w), F32)
    k_new = jnp.concatenate([kn_ref[0], pad], axis=0).astype(BF16)
    v_new = jnp.concatenate([vn_ref[0], pad], axis=0).astype(BF16)

    m_el = None
    for p in range(n_pages + 1):
        k_blk = k_pages[p][...].astype(BF16) if p < n_pages else k_new
        s = _dot_nt(q_exp, k_blk) + bias_ref[p]
        s_ref[p] = s
        m_el = s if m_el is None else jnp.maximum(m_el, s)
    m = jnp.max(m_el, -1, keepdims=True)

    l_el = jnp.zeros((n_rows, PAGE_SIZE), F32)
    acc = jnp.zeros((n_rows, att_w), F32)
    for p in range(n_pages + 1):
        v_blk = v_pages[p][...].astype(BF16) if p < n_pages else v_new
        e = jnp.exp(s_ref[p] - m)
        l_el = l_el + e
        acc = acc + _dot(e.astype(BF16), v_blk)
    o_all = acc / jnp.sum(l_el, -1, keepdims=True)

    g = g_ref[...]
    heads = []
    for h in range(n_heads):
        r = 2 * h * dec_seq
        c = h * HEAD_WIDTH
        o = o_all[r:r + dec_seq, c:c + HEAD_WIDTH] - lam * o_all[r + dec_seq:r + 2 * dec_seq, c:c + HEAD_WIDTH]
        heads.append(o * lax.rsqrt(jnp.mean(o * o, -1, keepdims=True) + EPS) * g * sub_scale)
    o_ref[0] = jnp.concatenate(heads, axis=-1)


def _attn_sample(page_table, lam, q, k_new, v_new, bias, g, cache_k, cache_v, *, layer, sub_scale):
    dec_b, dec_seq, att_w = q.shape
    n_pages = page_table.shape[1]
    n_heads = att_w // HEAD_WIDTH
    n_rows = 2 * n_heads * dec_seq
    pt_flat = page_table.reshape(-1)
    per_b = pl.BlockSpec((1, dec_seq, att_w), lambda b, pt: (b, 0, 0))

    def page_spec(p):
        return pl.BlockSpec((None, None, PAGE_SIZE, att_w),
                            lambda b, pt: (layer, pt[b * n_pages + p], 0, 0))

    pages = [page_spec(p) for p in range(n_pages)]
    grid_spec = pltpu.PrefetchScalarGridSpec(
        num_scalar_prefetch=1,
        grid=(dec_b,),
        in_specs=[pl.BlockSpec(memory_space=pltpu.SMEM), per_b, per_b, per_b,
                  pl.BlockSpec(bias.shape, lambda b, pt: (0, 0, 0)),
                  pl.BlockSpec((1, HEAD_WIDTH), lambda b, pt: (0, 0))] + pages + pages,
        out_specs=per_b,
        scratch_shapes=[pltpu.VMEM((n_pages + 1, n_rows, PAGE_SIZE), F32)],
    )
    return pl.pallas_call(
        functools.partial(_attn_sample_kernel, n_pages=n_pages, n_heads=n_heads, dec_seq=dec_seq,
                          sub_scale=sub_scale),
        grid_spec=grid_spec,
        out_shape=jax.ShapeDtypeStruct(q.shape, F32),
        compiler_params=pltpu.CompilerParams(dimension_semantics=("parallel",),
                                             vmem_limit_bytes=V7X_VMEM_LIMIT),
        name="attn_sample",
    )(pt_flat, lam, q, k_new, v_new, bias, g, *([cache_k] * n_pages), *([cache_v] * n_pages))


def _ssm_kernel(u_ref, h0r_ref, h0i_ref, lbr_ref, lbi_ref, br_ref, bi_ref, cr_ref, ci_ref, d_ref,
                z_ref, hr_ref, hi_ref, u_tm, x_tm, z_tm, *, nb, tc):
    n_state = lbr_ref.shape[1]
    half_s = n_state // 2
    n_lane_chunks = u_tm.shape[0]
    half_c = n_lane_chunks // 2

    @pl.when(pl.program_id(0) == 0)
    def _():
        hr_ref[...] = h0r_ref[...]
        hi_ref[...] = h0i_ref[...]

    def to_time_major(b, c):
        ub = u_ref[b]
        for j in range(n_lane_chunks):
            u_tm[j, pl.ds(b, tc, stride=nb), :] = ub[:, j * LANES:(j + 1) * LANES]
        return c

    lax.fori_loop(0, nb, to_time_major, 0)

    def u_cols(j0, j1, rows=slice(None)):
        return jnp.concatenate([u_tm[j, rows, :] for j in range(j0, j1)], axis=-1)

    for half in range(2):
        uh = u_cols(half * half_c, (half + 1) * half_c).astype(BF16)
        x_tm[:, half * half_s:(half + 1) * half_s] = _dot(uh, br_ref[half])
        x_tm[:, n_state + half * half_s:n_state + (half + 1) * half_s] = _dot(uh, bi_ref[half])

    cw = 512
    for cg in range(n_state // cw):
        re = slice(cg * cw, (cg + 1) * cw)
        im = slice(n_state + cg * cw, n_state + (cg + 1) * cw)
        lr = lbr_ref[:, re]
        li = lbi_ref[:, re]

        def rows(rg, c, re=re, im=im, lr=lr, li=li):
            r0 = pl.multiple_of(rg * 8, 8)

            def step(t, h):
                hr, hi = h
                row = pl.multiple_of(t * nb + r0, 8)
                nr = lr * hr - li * hi + x_tm[pl.ds(row, 8), re]
                ni = lr * hi + li * hr + x_tm[pl.ds(row, 8), im]
                x_tm[pl.ds(row, 8), re] = nr
                x_tm[pl.ds(row, 8), im] = ni
                return nr, ni

            h = (hr_ref[pl.ds(r0, 8), re], hi_ref[pl.ds(r0, 8), re])
            hr, hi = lax.fori_loop(0, tc, step, h, unroll=8)
            hr_ref[pl.ds(r0, 8), re] = hr
            hi_ref[pl.ds(r0, 8), re] = hi
            return c

        lax.fori_loop(0, nb // 8, rows, 0)

    n_rows = nb * tc
    rc = n_rows if n_rows <= 512 else 256
    for r0 in range(0, n_rows, rc):
        hb = x_tm[r0:r0 + rc, :].astype(BF16)
        ys = []
        for half in range(2):
            y = _dot(hb[:, half * half_s:(half + 1) * half_s], cr_ref[half])
            y = y + _dot(hb[:, n_state + half * half_s:n_state + (half + 1) * half_s], ci_ref[half])
            ys.append(y)
        y = jnp.concatenate(ys, axis=-1) + d_ref[...] * u_cols(0, n_lane_chunks, slice(r0, r0 + rc))
        z = jax.nn.gelu(y)
        for j in range(n_lane_chunks):
            z_tm[j, r0:r0 + rc, :] = z[:, j * LANES:(j + 1) * LANES]

    def to_batch_major(b, c):
        z_ref[b] = jnp.concatenate(
            [z_tm[j, pl.ds(b, tc, stride=nb), :] for j in range(n_lane_chunks)], axis=-1)
        return c

    lax.fori_loop(0, nb, to_batch_major, 0)


def _ssm(u, h0_re, h0_im, lb_re, lb_im, b_re, b_im, c_re, c_im, d_vec, *, tc):
    nb, t, ssm_w = u.shape
    n_state = lb_re.shape[1]
    assert t % tc == 0 and nb % 8 == 0
    state = pl.BlockSpec((nb, n_state), lambda i: (0, 0))
    seq = pl.BlockSpec((nb, tc, ssm_w), lambda i: (0, i, 0))
    whole = lambda a: pl.BlockSpec(a.shape, lambda i: (0,) * a.ndim)
    return pl.pallas_call(
        functools.partial(_ssm_kernel, nb=nb, tc=tc),
        grid=(t // tc,),
        in_specs=[seq, state, state, whole(lb_re), whole(lb_im), whole(b_re), whole(b_im),
                  whole(c_re), whole(c_im), whole(d_vec)],
        out_specs=[seq, state, state],
        out_shape=[jax.ShapeDtypeStruct(u.shape, F32), jax.ShapeDtypeStruct((nb, n_state), F32),
                   jax.ShapeDtypeStruct((nb, n_state), F32)],
        scratch_shapes=[pltpu.VMEM((ssm_w // LANES, nb * tc, LANES), F32),
                        pltpu.VMEM((nb * tc, 2 * n_state), F32),
                        pltpu.VMEM((ssm_w // LANES, nb * tc, LANES), F32)],
        compiler_params=pltpu.CompilerParams(dimension_semantics=("arbitrary",),
                                             vmem_limit_bytes=V7X_VMEM_LIMIT),
        name="ssm",
    )(u, h0_re, h0_im, lb_re, lb_im, b_re, b_im, c_re, c_im, d_vec)


def _merge_kernel(x_ref, on_ref, z_ref, ga_ref, gb_ref, wau_ref, wglu_ref, wout_ref, g_ref, b_ref,
                  o_ref, *, alpha):
    d = x_ref.shape[1]
    a_branch = _dot(on_ref[...].astype(BF16), wau_ref[...])
    glu = _dot(z_ref[...].astype(BF16), wglu_ref[...])
    b_branch = glu[:, :d] * jax.nn.sigmoid(glu[:, d:])
    mix_in = (ga_ref[...] * a_branch + gb_ref[...] * b_branch).astype(BF16)
    mix = _dot(mix_in, wout_ref[...])
    o_ref[...] = _layer_norm(alpha * x_ref[...] + mix, g_ref[...], b_ref[...])


def _merge(x, o_n, z, sga, sgb, w_att_up, w_glu, w_out, ln_g, ln_b, *, tm, alpha):
    n, d = x.shape
    row = lambda a: pl.BlockSpec((tm, a.shape[1]), lambda i: (i, 0))
    return pl.pallas_call(
        functools.partial(_merge_kernel, alpha=alpha),
        grid=(n // tm,),
        in_specs=[row(x), row(o_n), row(z), row(sga), row(sgb), _resident(w_att_up.shape),
                  _resident(w_glu.shape), _resident(w_out.shape), _resident((1, d)), _resident((1, d))],
        out_specs=row(x),
        out_shape=jax.ShapeDtypeStruct(x.shape, F32),
        compiler_params=pltpu.CompilerParams(dimension_semantics=("parallel",),
                                             vmem_limit_bytes=V7X_VMEM_LIMIT),
        name="merge",
    )(x, o_n, z, sga, sgb, w_att_up, w_glu, w_out, ln_g, ln_b)


def _ffn_kernel(x_ref, wup_ref, wdn_ref, g_ref, b_ref, o_ref, *, alpha, ff_chunk):
    x = x_ref[...]
    xb = x.astype(BF16)
    acc = alpha * x
    for c0 in range(0, wup_ref.shape[1], ff_chunk):
        hid = jnp.maximum(_dot(xb, wup_ref[:, c0:c0 + ff_chunk]), 0.0)
        acc = acc + _dot((hid * hid).astype(BF16), wdn_ref[c0:c0 + ff_chunk, :])
    o_ref[...] = _layer_norm(acc, g_ref[...], b_ref[...])


def _ffn(x, w_up, w_down, ln_g, ln_b, *, tm, alpha):
    n, d = x.shape
    row = pl.BlockSpec((tm, d), lambda i: (i, 0))
    return pl.pallas_call(
        functools.partial(_ffn_kernel, alpha=alpha, ff_chunk=d),
        grid=(n // tm,),
        in_specs=[row, _resident(w_up.shape), _resident(w_down.shape), _resident((1, d)),
                  _resident((1, d))],
        out_specs=row,
        out_shape=jax.ShapeDtypeStruct(x.shape, F32),
        compiler_params=pltpu.CompilerParams(dimension_semantics=("parallel",),
                                             vmem_limit_bytes=V7X_VMEM_LIMIT),
        name="ffn",
    )(x, w_up, w_down, ln_g, ln_b)


def _t5_bucket(rel):
    n = jnp.maximum(rel, 0)
    exact = N_BUCKETS // 2
    large = exact + (jnp.log(jnp.maximum(n, 1).astype(F32) / exact)
                     / math.log(MAX_DISTANCE / exact) * (N_BUCKETS - exact)).astype(jnp.int32)
    return jnp.where(n < exact, n, jnp.minimum(large, N_BUCKETS - 1))


def _bias_tile(rel_bias, rel, masked):
    b = jnp.transpose(rel_bias.astype(F32)[_t5_bucket(rel)], (2, 0, 1))
    return jnp.where(rel[None] >= 0, b, -jnp.inf) if masked else b


def _bias_tables(rel_bias, tq, dec_seq, past):
    i32 = jnp.int32
    r = jnp.arange(tq, dtype=i32)[:, None]
    c = jnp.arange(tq, dtype=i32)[None, :]
    d0 = _bias_tile(rel_bias, r - c, True)
    d1 = _bias_tile(rel_bias, tq + r - c, False)
    mb = _bias_tile(rel_bias, N_META + r - jnp.arange(N_META, dtype=i32)[None, :], False)
    mr = jnp.arange(N_META, dtype=i32)[:, None]
    mm = _bias_tile(rel_bias, mr - jnp.arange(PAGE_SIZE, dtype=i32)[None, :], True)
    b_far = _bias_tile(rel_bias, jnp.full((1, 1), MAX_DISTANCE + 1, i32), False)[:, 0, 0]
    n_heads = rel_bias.shape[1]
    s = jnp.arange(dec_seq, dtype=i32)[:, None]
    kk = jnp.arange(PAGE_SIZE, dtype=i32)[None, :]
    n_pages = past // PAGE_SIZE

    def rows(tile):
        return jnp.broadcast_to(tile[:, None], (n_heads, 2, dec_seq, PAGE_SIZE)).reshape(-1, PAGE_SIZE)

    far = rows(jnp.broadcast_to(b_far[:, None, None], (n_heads, dec_seq, PAGE_SIZE)))
    last = rows(_bias_tile(rel_bias, PAGE_SIZE + s - kk, False))
    new = rows(_bias_tile(rel_bias, jnp.where(kk < dec_seq, s - kk, -1), True))
    sample = jnp.stack([far] * (n_pages - 1) + [last, new])
    return d0, d1, mb, mm, b_far, sample


def _ssm_tables(a_re, a_im, log_dt, b_re, b_im, c_re, c_im, d):
    n_groups, n_p = a_re.shape
    a_re = a_re.astype(F32)
    a_im = a_im.astype(F32)
    dt = jnp.exp(log_dt.astype(F32))[:, None]
    mag = jnp.exp(a_re * dt)
    lb_re = mag * jnp.cos(a_im * dt)
    lb_im = mag * jnp.sin(a_im * dt)
    nr, ni = lb_re - 1.0, lb_im
    den = a_re * a_re + a_im * a_im
    f_re = (nr * a_re + ni * a_im) / den
    f_im = (ni * a_re - nr * a_im) / den
    b_re = b_re.astype(F32)
    b_im = b_im.astype(F32)
    bb_re = f_re[..., None] * b_re - f_im[..., None] * b_im
    bb_im = f_re[..., None] * b_im + f_im[..., None] * b_re
    eye = jnp.eye(n_groups, dtype=F32)
    hg = n_groups // 2

    def in_mat(bb):
        full = jnp.einsum('gpc,gh->gchp', bb, eye).reshape(n_groups * GROUP_CH, n_groups * n_p)
        hu, hs = hg * GROUP_CH, hg * n_p
        return jnp.stack([full[:hu, :hs], full[hu:, hs:]]).astype(BF16)

    def out_mat(cc):
        full = jnp.einsum('gcp,gh->gphc', cc, eye).reshape(n_groups * n_p, n_groups * GROUP_CH)
        hu, hs = hg * GROUP_CH, hg * n_p
        return jnp.stack([full[:hs, :hu], full[hs:, hu:]]).astype(BF16)

    lb = (jnp.broadcast_to(lb_re.reshape(1, -1), (8, n_groups * n_p)),
          jnp.broadcast_to(lb_im.reshape(1, -1), (8, n_groups * n_p)))
    return (lb, in_mat(bb_re), in_mat(bb_im), out_mat(c_re.astype(F32)), out_mat(-c_im.astype(F32)),
            d.astype(F32).reshape(1, -1))


def kernel(x_prompt, x_sample, cache_k, cache_v, state_ssm_re, state_ssm_im, page_table, meta_tokens, ln_in_g, ln_in_b, rel_bias, w_in, lambda_q1, lambda_k1, lambda_q2, lambda_k2, subln_g, w_att_up, ssm_a_re, ssm_a_im, ssm_log_dt, ssm_b_re, ssm_b_im, ssm_c_re, ssm_c_im, ssm_d, w_glu, w_out, ln1_g, ln1_b, w_up, w_down, ln2_g, ln2_b):
    bsz, seq, d_model = x_prompt.shape
    dec_b, dec_seq, _ = x_sample.shape
    depth = w_in.shape[0]
    att_w = w_att_up.shape[1]
    ssm_w = w_glu.shape[1]
    n_heads = att_w // HEAD_WIDTH
    n_groups = ssm_w // GROUP_CH
    n_state = n_groups * STATE_DIM
    t = N_META + seq
    past = page_table.shape[1] * PAGE_SIZE
    alpha = (2.0 * depth) ** 0.25
    assert seq % ATT_Q_TILE == 0 and t % SSM_CHUNK == 0 and past >= 2 * PAGE_SIZE

    tm_p = _row_tile(t, 704)
    tm_s = _row_tile(dec_b * dec_seq, 512)

    meta = jnp.broadcast_to(meta_tokens[None].astype(x_prompt.dtype), (bsz, N_META, d_model))
    xp = jnp.concatenate([meta, x_prompt], axis=1).reshape(bsz * t, d_model)
    xs = x_sample.reshape(dec_b * dec_seq, d_model)
    row = lambda a: a.astype(F32).reshape(1, -1)
    d0, d1, mb, mm, b_far, bias_s = _bias_tables(rel_bias, ATT_Q_TILE, dec_seq, past)
    n_pool = cache_k.shape[1]
    cache_k = cache_k.reshape(depth, n_pool, PAGE_SIZE, att_w)
    cache_v = cache_v.reshape(depth, n_pool, PAGE_SIZE, att_w)
    h0p = jnp.zeros((bsz, n_state), F32)

    outs = {name: [] for name in ("kp", "vp", "hpr", "hpi", "ks", "vs", "hsr", "hsi")}
    for l in range(depth):
        lam_init = 0.8 - 0.6 * math.exp(-0.3 * l)
        lam = (jnp.exp(jnp.sum(lambda_q1[l].astype(F32) * lambda_k1[l].astype(F32)))
               - jnp.exp(jnp.sum(lambda_q2[l].astype(F32) * lambda_k2[l].astype(F32))) + lam_init)
        lam = lam.reshape(1)
        sub_scale = 1.0 - lam_init
        g_sub = row(subln_g[l])
        w_in_bf = w_in[l].astype(BF16)
        w_au_bf = w_att_up[l].astype(BF16)
        w_glu_bf = w_glu[l].astype(BF16)
        w_out_bf = w_out[l].astype(BF16)
        w_up_bf = w_up[l].astype(BF16)
        w_dn_bf = w_down[l].astype(BF16)
        lb, sb_re, sb_im, sc_re, sc_im, d_vec = _ssm_tables(
            ssm_a_re[l], ssm_a_im[l], ssm_log_dt[l], ssm_b_re[l], ssm_b_im[l], ssm_c_re[l],
            ssm_c_im[l], ssm_d[l])
        ln = dict(apply_ln=(l == 0), att_w=att_w, ssm_w=ssm_w)

        xp, q, kf, kb, vf, vb, u, sga, sgb = _in_proj(xp, row(ln_in_g), row(ln_in_b), w_in_bf,
                                                      tm=tm_p, **ln)
        seq3 = lambda a: a.reshape(bsz, t, a.shape[-1])
        o_n = _attn_prompt(lam, b_far, seq3(q), seq3(kb), seq3(vb), d0, d1, mb, mm, g_sub,
                           sub_scale=sub_scale)
        z, hpr, hpi = _ssm(seq3(u), h0p, h0p, *lb, sb_re, sb_im, sc_re, sc_im, d_vec,
                           tc=SSM_CHUNK)
        xp = _merge(xp, o_n.reshape(bsz * t, att_w), z.reshape(bsz * t, ssm_w), sga, sgb, w_au_bf,
                    w_glu_bf, w_out_bf, row(ln1_g[l]), row(ln1_b[l]), tm=tm_p, alpha=alpha)
        xp = _ffn(xp, w_up_bf, w_dn_bf, row(ln2_g[l]), row(ln2_b[l]), tm=tm_p, alpha=alpha)
        outs["kp"].append(kf.reshape(bsz, t, n_heads, 2, HEAD_DIM))
        outs["vp"].append(vf.reshape(bsz, t, n_heads, HEAD_WIDTH))
        outs["hpr"].append(hpr.reshape(bsz, n_groups, STATE_DIM))
        outs["hpi"].append(hpi.reshape(bsz, n_groups, STATE_DIM))

        xs, q, kf, kb, vf, vb, u, sga, sgb = _in_proj(xs, row(ln_in_g), row(ln_in_b), w_in_bf,
                                                      tm=tm_s, **ln)
        dec3 = lambda a: a.reshape(dec_b, dec_seq, a.shape[-1])
        o_n = _attn_sample(page_table, lam, dec3(q.astype(F32)), dec3(kf), dec3(vf), bias_s, g_sub, cache_k,
                           cache_v, layer=l, sub_scale=sub_scale)
        z, hsr, hsi = _ssm(dec3(u), state_ssm_re[l].reshape(dec_b, n_state).astype(F32),
                           state_ssm_im[l].reshape(dec_b, n_state).astype(F32), *lb, sb_re,
                           sb_im, sc_re, sc_im, d_vec, tc=dec_seq)
        xs = _merge(xs, o_n.reshape(dec_b * dec_seq, att_w), z.reshape(dec_b * dec_seq, ssm_w), sga,
                    sgb, w_au_bf, w_glu_bf, w_out_bf, row(ln1_g[l]), row(ln1_b[l]), tm=tm_s,
                    alpha=alpha)
        xs = _ffn(xs, w_up_bf, w_dn_bf, row(ln2_g[l]), row(ln2_b[l]), tm=tm_s, alpha=alpha)
        outs["ks"].append(kf.reshape(dec_b, dec_seq, n_heads, 2, HEAD_DIM))
        outs["vs"].append(vf.reshape(dec_b, dec_seq, n_heads, HEAD_WIDTH))
        outs["hsr"].append(hsr.reshape(dec_b, n_groups, STATE_DIM))
        outs["hsi"].append(hsi.reshape(dec_b, n_groups, STATE_DIM))

    y_prompt = xp.reshape(bsz, t, d_model)[:, N_META:]
    y_sample = xs.reshape(dec_b, dec_seq, d_model)
    st = lambda name: jnp.stack(outs[name])
    return (y_prompt, y_sample, st("kp"), st("vp"), st("hpr"), st("hpi"), st("ks"), st("vs"),
            st("hsr"), st("hsi"))
```

```python
import functools
import math

import jax
import jax.numpy as jnp
from jax import lax
from jax.experimental import pallas as pl
from jax.experimental.pallas import tpu as pltpu

F32 = jnp.float32
BF16 = jnp.bfloat16

N_META = 16
HEAD_DIM = 64
HEAD_WIDTH = 2 * HEAD_DIM
GROUP_CH = 16
STATE_DIM = 64
PAGE_SIZE = 128
LANES = 128
N_BUCKETS = 32
MAX_DISTANCE = 128
EPS = 1e-5

ATT_Q_TILE = 256
SSM_CHUNK = 48
V7X_VMEM_LIMIT = 56 * 1024 * 1024

assert (N_BUCKETS // 2 + int(math.log((MAX_DISTANCE + 1) / (N_BUCKETS // 2))
                             / math.log(MAX_DISTANCE / (N_BUCKETS // 2)) * (N_BUCKETS // 2))
        >= N_BUCKETS - 1)


def _layer_norm(x, g, b):
    mu = jnp.mean(x, -1, keepdims=True)
    xc = x - mu
    var = jnp.mean(xc * xc, -1, keepdims=True)
    return xc * lax.rsqrt(var + EPS) * g + b


def _dot(a, b):
    return jnp.dot(a, b, preferred_element_type=F32)


def _dot_nt(a, b):
    return lax.dot_general(a, b, (((1,), (1,)), ((), ())), preferred_element_type=F32)


def _resident(shape):
    return pl.BlockSpec(shape, lambda *_: (0,) * len(shape), pipeline_mode=pl.Buffered(1))


def _row_tile(n_rows, target):
    best = None
    for t in range(16, min(n_rows, target) + 1, 16):
        if n_rows % t == 0:
            best = t
    assert best is not None, n_rows
    return best


def _in_proj_kernel(x_ref, g_ref, b_ref, w_ref, xo_ref, q_ref, kf_ref, kb_ref, vf_ref, vb_ref,
                    u_ref, ga_ref, gb_ref, *, apply_ln, att_w, ssm_w, d_model):
    x = x_ref[...]
    if apply_ln:
        x = _layer_norm(x, g_ref[...], b_ref[...])
    xo_ref[...] = x
    xb = x.astype(BF16)
    c0 = 0

    def mm(width):
        nonlocal c0
        r = _dot(xb, w_ref[:, c0:c0 + width])
        c0 += width
        return r

    q_ref[...] = (mm(att_w) * (HEAD_DIM ** -0.5)).astype(BF16)
    k = mm(att_w)
    kf_ref[...] = k
    kb_ref[...] = k.astype(BF16)
    v = mm(att_w)
    vf_ref[...] = v
    vb_ref[...] = v.astype(BF16)
    u_ref[...] = mm(ssm_w)
    ga_ref[...] = jax.nn.sigmoid(mm(d_model))
    gb_ref[...] = jax.nn.sigmoid(mm(d_model))


def _in_proj(x, ln_g, ln_b, w_in_bf, *, apply_ln, tm, att_w, ssm_w):
    n, d = x.shape
    in_w = w_in_bf.shape[1]
    row = lambda w: pl.BlockSpec((tm, w), lambda i: (i, 0))
    outs = [(d, F32), (att_w, BF16), (att_w, F32), (att_w, BF16), (att_w, F32), (att_w, BF16),
            (ssm_w, F32), (d, F32), (d, F32)]
    return pl.pallas_call(
        functools.partial(_in_proj_kernel, apply_ln=apply_ln, att_w=att_w, ssm_w=ssm_w, d_model=d),
        grid=(n // tm,),
        in_specs=[row(d), _resident((1, d)), _resident((1, d)), _resident((d, in_w))],
        out_specs=[row(w) for w, _ in outs],
        out_shape=[jax.ShapeDtypeStruct((n, w), dt) for w, dt in outs],
        compiler_params=pltpu.CompilerParams(dimension_semantics=("parallel",),
                                             vmem_limit_bytes=V7X_VMEM_LIMIT),
        name="in_proj",
    )(x, ln_g, ln_b, w_in_bf)


def _bias_kernel(rb_ref, rel_ref, o_ref):
    rel = rel_ref[...]
    n = jnp.maximum(rel, 0)
    exact = N_BUCKETS // 2
    large = exact + (jnp.log(jnp.maximum(n, 1).astype(F32) / exact)
                     / math.log(MAX_DISTANCE / exact) * (N_BUCKETS - exact)).astype(jnp.int32)
    bucket = jnp.where(n < exact, n, jnp.minimum(large, N_BUCKETS - 1))
    for h in range(o_ref.shape[0]):
        far = rb_ref[N_BUCKETS - 1, h]
        acc = jnp.zeros(rel.shape, F32)
        for b in range(N_BUCKETS - 1):
            acc = jnp.where(bucket == b, rb_ref[b, h] - far, acc)
        o_ref[h] = jnp.where(rel >= 0, acc, -jnp.inf)


def _bias_tile(rel_bias, rel):
    n_heads = rel_bias.shape[1]
    return pl.pallas_call(
        _bias_kernel,
        in_specs=[pl.BlockSpec(memory_space=pltpu.SMEM), pl.BlockSpec(rel.shape, lambda: (0, 0))],
        out_specs=pl.BlockSpec((n_heads,) + rel.shape, lambda: (0, 0, 0)),
        out_shape=jax.ShapeDtypeStruct((n_heads,) + rel.shape, F32),
        name="bias_tile",
    )(rel_bias.astype(F32), rel)


def _bias_tables(rel_bias, tq, dec_seq):
    i32 = jnp.int32
    n_heads = rel_bias.shape[1]
    r = jnp.arange(tq, dtype=i32)[:, None]
    near = _bias_tile(rel_bias, tq + r - jnp.arange(2 * tq, dtype=i32)[None, :])
    mb = _bias_tile(rel_bias, N_META + r - jnp.arange(N_META, dtype=i32)[None, :])
    mr = jnp.arange(N_META, dtype=i32)
    mm = _bias_tile(rel_bias, mr[:, None] - mr[None, :])
    s = jnp.arange(dec_seq, dtype=i32)[:, None]
    kk = jnp.arange(PAGE_SIZE, dtype=i32)[None, :]
    last = _bias_tile(rel_bias, PAGE_SIZE + s - kk)
    new = _bias_tile(rel_bias, jnp.where(kk < dec_seq, s - kk, -1))

    def rows(tile):
        return jnp.broadcast_to(tile[:, None], (n_heads, 2, dec_seq, PAGE_SIZE)).reshape(-1, PAGE_SIZE)

    return near, mb, mm, jnp.stack([rows(last), rows(new)])


def _softmax_pv(blocks):
    m = functools.reduce(jnp.maximum, [jnp.max(s, -1, keepdims=True) for s, _ in blocks])
    l = 0.0
    acc = 0.0
    for s, v in blocks:
        p = jnp.exp(s - m)
        l = l + jnp.sum(p, -1, keepdims=True)
        acc = acc + _dot(p.astype(BF16), v)
    return acc / l


def _sub_norm(o1, o2, lam, g, sub_scale):
    o = o1 - lam * o2
    return o * lax.rsqrt(jnp.mean(o * o, -1, keepdims=True) + EPS) * g * sub_scale


def _attn_prompt_kernel(lam_ref, q_ref, k_ref, v_ref, near_ref, mb_ref, mm_ref, g_ref, o_ref,
                        k1_ref, k2_ref, *, tq, n_q, sub_scale):
    lam = lam_ref[0]
    g = g_ref[...]
    k = k_ref[0]
    lane = lax.broadcasted_iota(jnp.int32, k.shape, 1)
    zero = jnp.zeros_like(k)
    k1_ref[...] = jnp.where(lane < HEAD_DIM, k, zero)
    k2_ref[...] = jnp.where(lane >= HEAD_DIM, k, zero)

    def attend(q_t, spans):
        outs = []
        for kc_ref in (k1_ref, k2_ref):
            blocks = []
            for k0, size, bias in spans:
                s = _dot_nt(q_t, kc_ref[k0:k0 + size, :])
                blocks.append((s if bias is None else s + bias, v_ref[0, k0:k0 + size, :]))
            outs.append(_softmax_pv(blocks))
        return _sub_norm(outs[0], outs[1], lam, g, sub_scale).astype(BF16)

    o_ref[0, 0:N_META, :] = attend(q_ref[0, 0:N_META, :], [(0, N_META, mm_ref[0])])
    for i in range(n_q):
        r0 = N_META + i * tq
        q_t = q_ref[0, r0:r0 + tq, :]
        if i == 0:
            spans = [(0, N_META, mb_ref[0]), (r0, tq, near_ref[0, :, tq:])]
        else:
            spans = [(0, r0 - tq, None), (r0 - tq, 2 * tq, near_ref[0])]
        o_ref[0, r0:r0 + tq, :] = attend(q_t, spans)


def _attn_prompt(lam, q, kb, vb, near, mb, mm, g, *, sub_scale):
    bsz, t, att_w = q.shape
    n_heads = att_w // HEAD_WIDTH
    tq = near.shape[1]
    n_q = (t - N_META) // tq
    assert n_q * tq + N_META == t and tq > MAX_DISTANCE
    smem = pl.BlockSpec(memory_space=pltpu.SMEM)
    seq = pl.BlockSpec((1, t, HEAD_WIDTH), lambda b, h: (b, 0, h))
    per_head = lambda a: pl.BlockSpec((1,) + a.shape[1:], lambda b, h: (h, 0, 0))
    return pl.pallas_call(
        functools.partial(_attn_prompt_kernel, tq=tq, n_q=n_q, sub_scale=sub_scale),
        grid=(bsz, n_heads),
        in_specs=[smem, seq, seq, seq, per_head(near), per_head(mb), per_head(mm),
                  pl.BlockSpec((1, HEAD_WIDTH), lambda b, h: (0, 0))],
        out_specs=seq,
        out_shape=jax.ShapeDtypeStruct(q.shape, BF16),
        scratch_shapes=[pltpu.VMEM((t, HEAD_WIDTH), BF16), pltpu.VMEM((t, HEAD_WIDTH), BF16)],
        compiler_params=pltpu.CompilerParams(dimension_semantics=("parallel", "parallel"),
                                             vmem_limit_bytes=V7X_VMEM_LIMIT),
        name="attn_prompt",
    )(lam, q, kb, vb, near, mb, mm, g)


def _attn_sample_kernel(pt_ref, lam_ref, q_ref, kn_ref, vn_ref, bias_ref, g_ref, *rest,
                        n_pages, n_heads, dec_seq, sub_scale):
    kt_pages = rest[:n_pages]
    v_pages = rest[n_pages:2 * n_pages]
    o_ref, s_ref = rest[2 * n_pages:]
    del pt_ref
    lam = lam_ref[0]
    att_w = q_ref.shape[2]
    n_rows = 2 * n_heads * dec_seq
    head_rows = 2 * dec_seq
    q = q_ref[0]
    q_rep = jnp.concatenate([q] * (2 * n_heads), axis=0)
    row_blk = lax.broadcasted_iota(jnp.int32, (n_rows, att_w), 0) // dec_seq
    col_blk = lax.broadcasted_iota(jnp.int32, (n_rows, att_w), 1) // HEAD_DIM
    q_exp = jnp.where(row_blk == col_blk, q_rep, 0.0).astype(BF16)

    pad = jnp.zeros((PAGE_SIZE - dec_seq, att_w), F32)
    k_new = jnp.concatenate([kn_ref[0], pad], axis=0).astype(BF16)
    v_new = jnp.concatenate([vn_ref[0], pad], axis=0).astype(BF16)

    m_el = None
    for p in range(n_pages + 1):
        if p < n_pages:
            s = _dot(q_exp, kt_pages[p][...].astype(BF16))
            if p == n_pages - 1:
                s = s + bias_ref[0]
        else:
            s = _dot_nt(q_exp, k_new) + bias_ref[1]
        s_ref[p] = s
        m_el = s if m_el is None else jnp.maximum(m_el, s)
    m = jnp.max(m_el, -1, keepdims=True)

    l_el = jnp.zeros((n_rows, PAGE_SIZE), F32)
    accs = [jnp.zeros((head_rows, HEAD_WIDTH), F32) for _ in range(n_heads)]
    for p in range(n_pages + 1):
        e = jnp.exp(s_ref[p] - m)
        l_el = l_el + e
        eb = e.astype(BF16)
        for h in range(n_heads):
            if p < n_pages:
                v_h = v_pages[p][pl.ds(h, PAGE_SIZE, stride=n_heads), :].astype(BF16)
            else:
                v_h = v_new[:, h * HEAD_WIDTH:(h + 1) * HEAD_WIDTH]
            accs[h] = accs[h] + _dot(eb[h * head_rows:(h + 1) * head_rows, :], v_h)
    l = jnp.sum(l_el, -1, keepdims=True)

    g = g_ref[...]
    heads = []
    for h in range(n_heads):
        o = accs[h] / l[h * head_rows:(h + 1) * head_rows]
        heads.append(_sub_norm(o[:dec_seq], o[dec_seq:], lam, g, sub_scale))
    o_ref[0] = jnp.concatenate(heads, axis=-1)


def _attn_sample(page_table, lam, q, k_new, v_new, bias, g, cache_kt, cache_v, *, layer, sub_scale):
    dec_b, dec_seq, att_w = q.shape
    n_pages = page_table.shape[1]
    n_heads = att_w // HEAD_WIDTH
    n_rows = 2 * n_heads * dec_seq
    pt_flat = page_table.reshape(-1)
    per_b = pl.BlockSpec((1, dec_seq, att_w), lambda b, pt: (b, 0, 0))

    def page_spec(p):
        return pl.BlockSpec((None, None) + cache_v.shape[2:],
                            lambda b, pt: (layer, pt[b * n_pages + p], 0, 0))

    pages = [page_spec(p) for p in range(n_pages)]
    grid_spec = pltpu.PrefetchScalarGridSpec(
        num_scalar_prefetch=1,
        grid=(dec_b,),
        in_specs=[pl.BlockSpec(memory_space=pltpu.SMEM), per_b, per_b, per_b,
                  pl.BlockSpec(bias.shape, lambda b, pt: (0, 0, 0)),
                  pl.BlockSpec((1, HEAD_WIDTH), lambda b, pt: (0, 0))] + pages + pages,
        out_specs=per_b,
        scratch_shapes=[pltpu.VMEM((n_pages + 1, n_rows, PAGE_SIZE), F32)],
    )
    return pl.pallas_call(
        functools.partial(_attn_sample_kernel, n_pages=n_pages, n_heads=n_heads, dec_seq=dec_seq,
                          sub_scale=sub_scale),
        grid_spec=grid_spec,
        out_shape=jax.ShapeDtypeStruct(q.shape, F32),
        compiler_params=pltpu.CompilerParams(dimension_semantics=("parallel",),
                                             vmem_limit_bytes=V7X_VMEM_LIMIT),
        name="attn_sample",
    )(pt_flat, lam, q, k_new, v_new, bias, g, *([cache_kt] * n_pages), *([cache_v] * n_pages))


def _ssm_kernel(u_ref, h0r_ref, h0i_ref, lbr_ref, lbi_ref, br_ref, bi_ref, cr_ref, ci_ref, d_ref,
                z_ref, hr_ref, hi_ref, u_tm, x_tm, z_tm, *, nb, tc):
    n_state = lbr_ref.shape[1]
    half_s = n_state // 2
    n_lane_chunks = u_tm.shape[0]
    half_c = n_lane_chunks // 2

    @pl.when(pl.program_id(0) == 0)
    def _():
        hr_ref[...] = h0r_ref[...]
        hi_ref[...] = h0i_ref[...]

    def to_time_major(b, c):
        ub = u_ref[b]
        for j in range(n_lane_chunks):
            u_tm[j, pl.ds(b, tc, stride=nb), :] = ub[:, j * LANES:(j + 1) * LANES]
        return c

    lax.fori_loop(0, nb, to_time_major, 0)

    def u_cols(j0, j1, rows=slice(None)):
        return jnp.concatenate([u_tm[j, rows, :] for j in range(j0, j1)], axis=-1)

    for half in range(2):
        uh = u_cols(half * half_c, (half + 1) * half_c).astype(BF16)
        x_tm[:, half * half_s:(half + 1) * half_s] = _dot(uh, br_ref[half])
        x_tm[:, n_state + half * half_s:n_state + (half + 1) * half_s] = _dot(uh, bi_ref[half])

    cw = 512
    for cg in range(n_state // cw):
        re = slice(cg * cw, (cg + 1) * cw)
        im = slice(n_state + cg * cw, n_state + (cg + 1) * cw)
        lr = lbr_ref[:, re]
        li = lbi_ref[:, re]

        def rows(rg, c, re=re, im=im, lr=lr, li=li):
            r0 = pl.multiple_of(rg * 8, 8)

            def step(t, h):
                hr, hi = h
                row = pl.multiple_of(t * nb + r0, 8)
                nr = lr * hr - li * hi + x_tm[pl.ds(row, 8), re]
                ni = lr * hi + li * hr + x_tm[pl.ds(row, 8), im]
                x_tm[pl.ds(row, 8), re] = nr
                x_tm[pl.ds(row, 8), im] = ni
                return nr, ni

            h = (hr_ref[pl.ds(r0, 8), re], hi_ref[pl.ds(r0, 8), re])
            hr, hi = lax.fori_loop(0, tc, step, h, unroll=8)
            hr_ref[pl.ds(r0, 8), re] = hr
            hi_ref[pl.ds(r0, 8), re] = hi
            return c

        lax.fori_loop(0, nb // 8, rows, 0)

    n_rows = nb * tc
    rc = n_rows if n_rows <= 512 else 256
    for r0 in range(0, n_rows, rc):
        hb = x_tm[r0:r0 + rc, :].astype(BF16)
        ys = []
        for half in range(2):
            y = _dot(hb[:, half * half_s:(half + 1) * half_s], cr_ref[half])
            y = y + _dot(hb[:, n_state + half * half_s:n_state + (half + 1) * half_s], ci_ref[half])
            ys.append(y)
        y = jnp.concatenate(ys, axis=-1) + d_ref[...] * u_cols(0, n_lane_chunks, slice(r0, r0 + rc))
        z = jax.nn.gelu(y)
        for j in range(n_lane_chunks):
            z_tm[j, r0:r0 + rc, :] = z[:, j * LANES:(j + 1) * LANES]

    def to_batch_major(b, c):
        z_ref[b] = jnp.concatenate(
            [z_tm[j, pl.ds(b, tc, stride=nb), :] for j in range(n_lane_chunks)], axis=-1)
        return c

    lax.fori_loop(0, nb, to_batch_major, 0)


def _ssm(u, h0_re, h0_im, lb_re, lb_im, b_re, b_im, c_re, c_im, d_vec, *, tc):
    nb, t, ssm_w = u.shape
    n_state = lb_re.shape[1]
    assert t % tc == 0 and nb % 8 == 0
    state = pl.BlockSpec((nb, n_state), lambda i: (0, 0))
    seq = pl.BlockSpec((nb, tc, ssm_w), lambda i: (0, i, 0))
    whole = lambda a: pl.BlockSpec(a.shape, lambda i: (0,) * a.ndim)
    return pl.pallas_call(
        functools.partial(_ssm_kernel, nb=nb, tc=tc),
        grid=(t // tc,),
        in_specs=[seq, state, state, whole(lb_re), whole(lb_im), whole(b_re), whole(b_im),
                  whole(c_re), whole(c_im), whole(d_vec)],
        out_specs=[seq, state, state],
        out_shape=[jax.ShapeDtypeStruct(u.shape, F32), jax.ShapeDtypeStruct((nb, n_state), F32),
                   jax.ShapeDtypeStruct((nb, n_state), F32)],
        scratch_shapes=[pltpu.VMEM((ssm_w // LANES, nb * tc, LANES), F32),
                        pltpu.VMEM((nb * tc, 2 * n_state), F32),
                        pltpu.VMEM((ssm_w // LANES, nb * tc, LANES), F32)],
        compiler_params=pltpu.CompilerParams(dimension_semantics=("arbitrary",),
                                             vmem_limit_bytes=V7X_VMEM_LIMIT),
        name="ssm",
    )(u, h0_re, h0_im, lb_re, lb_im, b_re, b_im, c_re, c_im, d_vec)


def _merge_kernel(x_ref, on_ref, z_ref, ga_ref, gb_ref, wau_ref, wglu_ref, wout_ref, g_ref, b_ref,
                  o_ref, *, alpha):
    d = x_ref.shape[1]
    a_branch = _dot(on_ref[...].astype(BF16), wau_ref[...])
    glu = _dot(z_ref[...].astype(BF16), wglu_ref[...])
    b_branch = glu[:, :d] * jax.nn.sigmoid(glu[:, d:])
    mix_in = (ga_ref[...] * a_branch + gb_ref[...] * b_branch).astype(BF16)
    mix = _dot(mix_in, wout_ref[...])
    o_ref[...] = _layer_norm(alpha * x_ref[...] + mix, g_ref[...], b_ref[...])


def _merge(x, o_n, z, sga, sgb, w_att_up, w_glu, w_out, ln_g, ln_b, *, tm, alpha):
    n, d = x.shape
    row = lambda a: pl.BlockSpec((tm, a.shape[1]), lambda i: (i, 0))
    return pl.pallas_call(
        functools.partial(_merge_kernel, alpha=alpha),
        grid=(n // tm,),
        in_specs=[row(x), row(o_n), row(z), row(sga), row(sgb), _resident(w_att_up.shape),
                  _resident(w_glu.shape), _resident(w_out.shape), _resident((1, d)), _resident((1, d))],
        out_specs=row(x),
        out_shape=jax.ShapeDtypeStruct(x.shape, F32),
        compiler_params=pltpu.CompilerParams(dimension_semantics=("parallel",),
                                             vmem_limit_bytes=V7X_VMEM_LIMIT),
        name="merge",
    )(x, o_n, z, sga, sgb, w_att_up, w_glu, w_out, ln_g, ln_b)


def _ffn_kernel(x_ref, wup_ref, wdn_ref, g_ref, b_ref, o_ref, *, alpha, ff_chunk):
    x = x_ref[...]
    xb = x.astype(BF16)
    acc = alpha * x
    for c0 in range(0, wup_ref.shape[1], ff_chunk):
        hid = jnp.maximum(_dot(xb, wup_ref[:, c0:c0 + ff_chunk]), 0.0)
        acc = acc + _dot((hid * hid).astype(BF16), wdn_ref[c0:c0 + ff_chunk, :])
    o_ref[...] = _layer_norm(acc, g_ref[...], b_ref[...])


def _ffn(x, w_up, w_down, ln_g, ln_b, *, tm, alpha):
    n, d = x.shape
    row = pl.BlockSpec((tm, d), lambda i: (i, 0))
    return pl.pallas_call(
        functools.partial(_ffn_kernel, alpha=alpha, ff_chunk=d),
        grid=(n // tm,),
        in_specs=[row, _resident(w_up.shape), _resident(w_down.shape), _resident((1, d)),
                  _resident((1, d))],
        out_specs=row,
        out_shape=jax.ShapeDtypeStruct(x.shape, F32),
        compiler_params=pltpu.CompilerParams(dimension_semantics=("parallel",),
                                             vmem_limit_bytes=V7X_VMEM_LIMIT),
        name="ffn",
    )(x, w_up, w_down, ln_g, ln_b)


def _ssm_tables(a_re, a_im, log_dt, b_re, b_im, c_re, c_im, d):
    n_groups, n_p = a_re.shape
    a_re = a_re.astype(F32)
    a_im = a_im.astype(F32)
    dt = jnp.exp(log_dt.astype(F32))[:, None]
    mag = jnp.exp(a_re * dt)
    lb_re = mag * jnp.cos(a_im * dt)
    lb_im = mag * jnp.sin(a_im * dt)
    nr, ni = lb_re - 1.0, lb_im
    den = a_re * a_re + a_im * a_im
    f_re = (nr * a_re + ni * a_im) / den
    f_im = (ni * a_re - nr * a_im) / den
    b_re = b_re.astype(F32)
    b_im = b_im.astype(F32)
    bb_re = f_re[..., None] * b_re - f_im[..., None] * b_im
    bb_im = f_re[..., None] * b_im + f_im[..., None] * b_re
    eye = jnp.eye(n_groups, dtype=F32)
    hg = n_groups // 2

    def in_mat(bb):
        full = jnp.einsum('gpc,gh->gchp', bb, eye).reshape(n_groups * GROUP_CH, n_groups * n_p)
        hu, hs = hg * GROUP_CH, hg * n_p
        return jnp.stack([full[:hu, :hs], full[hu:, hs:]]).astype(BF16)

    def out_mat(cc):
        full = jnp.einsum('gcp,gh->gphc', cc, eye).reshape(n_groups * n_p, n_groups * GROUP_CH)
        hu, hs = hg * GROUP_CH, hg * n_p
        return jnp.stack([full[:hs, :hu], full[hs:, hu:]]).astype(BF16)

    lb = (jnp.broadcast_to(lb_re.reshape(1, -1), (8, n_groups * n_p)),
          jnp.broadcast_to(lb_im.reshape(1, -1), (8, n_groups * n_p)))
    return (lb, in_mat(bb_re), in_mat(bb_im), out_mat(c_re.astype(F32)), out_mat(-c_im.astype(F32)),
            d.astype(F32).reshape(1, -1))


def kernel(x_prompt, x_sample, cache_k, cache_v, state_ssm_re, state_ssm_im, page_table, meta_tokens, ln_in_g, ln_in_b, rel_bias, w_in, lambda_q1, lambda_k1, lambda_q2, lambda_k2, subln_g, w_att_up, ssm_a_re, ssm_a_im, ssm_log_dt, ssm_b_re, ssm_b_im, ssm_c_re, ssm_c_im, ssm_d, w_glu, w_out, ln1_g, ln1_b, w_up, w_down, ln2_g, ln2_b):
    bsz, seq, d_model = x_prompt.shape
    dec_b, dec_seq, _ = x_sample.shape
    depth = w_in.shape[0]
    att_w = w_att_up.shape[1]
    ssm_w = w_glu.shape[1]
    n_heads = att_w // HEAD_WIDTH
    n_groups = ssm_w // GROUP_CH
    n_state = n_groups * STATE_DIM
    t = N_META + seq
    past = page_table.shape[1] * PAGE_SIZE
    alpha = (2.0 * depth) ** 0.25
    assert seq % ATT_Q_TILE == 0 and t % SSM_CHUNK == 0 and past >= 2 * PAGE_SIZE

    tm_p = _row_tile(t, 704)
    tm_s = _row_tile(dec_b * dec_seq, 512)

    meta = jnp.broadcast_to(meta_tokens[None].astype(x_prompt.dtype), (bsz, N_META, d_model))
    xp = jnp.concatenate([meta, x_prompt], axis=1).reshape(bsz * t, d_model)
    xs = x_sample.reshape(dec_b * dec_seq, d_model)
    row = lambda a: a.astype(F32).reshape(1, -1)
    near, mb, mm, bias_s = _bias_tables(rel_bias, ATT_Q_TILE, dec_seq)
    n_pool = cache_k.shape[1]
    cache_kt = jnp.transpose(cache_k, (0, 1, 3, 4, 5, 2)).reshape(depth, n_pool, att_w, PAGE_SIZE)
    cache_v = cache_v.reshape(depth, n_pool, PAGE_SIZE * n_heads, HEAD_WIDTH)
    h0p = jnp.zeros((bsz, n_state), F32)

    outs = {name: [] for name in ("kp", "vp", "hpr", "hpi", "ks", "vs", "hsr", "hsi")}
    for l in range(depth):
        lam_init = 0.8 - 0.6 * math.exp(-0.3 * l)
        lam = (jnp.exp(jnp.sum(lambda_q1[l].astype(F32) * lambda_k1[l].astype(F32)))
               - jnp.exp(jnp.sum(lambda_q2[l].astype(F32) * lambda_k2[l].astype(F32))) + lam_init)
        lam = lam.reshape(1)
        sub_scale = 1.0 - lam_init
        g_sub = row(subln_g[l])
        w_in_bf = w_in[l].astype(BF16)
        w_au_bf = w_att_up[l].astype(BF16)
        w_glu_bf = w_glu[l].astype(BF16)
        w_out_bf = w_out[l].astype(BF16)
        w_up_bf = w_up[l].astype(BF16)
        w_dn_bf = w_down[l].astype(BF16)
        lb, sb_re, sb_im, sc_re, sc_im, d_vec = _ssm_tables(
            ssm_a_re[l], ssm_a_im[l], ssm_log_dt[l], ssm_b_re[l], ssm_b_im[l], ssm_c_re[l],
            ssm_c_im[l], ssm_d[l])
        ln = dict(apply_ln=(l == 0), att_w=att_w, ssm_w=ssm_w)

        xp, q, kf, kb, vf, vb, u, sga, sgb = _in_proj(xp, row(ln_in_g), row(ln_in_b), w_in_bf,
                                                      tm=tm_p, **ln)
        seq3 = lambda a: a.reshape(bsz, t, a.shape[-1])
        o_n = _attn_prompt(lam, seq3(q), seq3(kb), seq3(vb), near, mb, mm, g_sub,
                           sub_scale=sub_scale)
        z, hpr, hpi = _ssm(seq3(u), h0p, h0p, *lb, sb_re, sb_im, sc_re, sc_im, d_vec,
                           tc=SSM_CHUNK)
        xp = _merge(xp, o_n.reshape(bsz * t, att_w), z.reshape(bsz * t, ssm_w), sga, sgb, w_au_bf,
                    w_glu_bf, w_out_bf, row(ln1_g[l]), row(ln1_b[l]), tm=tm_p, alpha=alpha)
        xp = _ffn(xp, w_up_bf, w_dn_bf, row(ln2_g[l]), row(ln2_b[l]), tm=tm_p, alpha=alpha)
        outs["kp"].append(kf.reshape(bsz, t, n_heads, 2, HEAD_DIM))
        outs["vp"].append(vf.reshape(bsz, t, n_heads, HEAD_WIDTH))
        outs["hpr"].append(hpr.reshape(bsz, n_groups, STATE_DIM))
        outs["hpi"].append(hpi.reshape(bsz, n_groups, STATE_DIM))

        xs, q, kf, kb, vf, vb, u, sga, sgb = _in_proj(xs, row(ln_in_g), row(ln_in_b), w_in_bf,
                                                      tm=tm_s, **ln)
        dec3 = lambda a: a.reshape(dec_b, dec_seq, a.shape[-1])
        o_n = _attn_sample(page_table, lam, dec3(q.astype(F32)), dec3(kf), dec3(vf), bias_s, g_sub,
                           cache_kt, cache_v, layer=l, sub_scale=sub_scale)
        z, hsr, hsi = _ssm(dec3(u), state_ssm_re[l].reshape(dec_b, n_state).astype(F32),
                           state_ssm_im[l].reshape(dec_b, n_state).astype(F32), *lb, sb_re,
                           sb_im, sc_re, sc_im, d_vec, tc=dec_seq)
        xs = _merge(xs, o_n.reshape(dec_b * dec_seq, att_w), z.reshape(dec_b * dec_seq, ssm_w), sga,
                    sgb, w_au_bf, w_glu_bf, w_out_bf, row(ln1_g[l]), row(ln1_b[l]), tm=tm_s,
                    alpha=alpha)
        xs = _ffn(xs, w_up_bf, w_dn_bf, row(ln2_g[l]), row(ln2_b[l]), tm=tm_s, alpha=alpha)
        outs["ks"].append(kf.reshape(dec_b, dec_seq, n_heads, 2, HEAD_DIM))
        outs["vs"].append(vf.reshape(dec_b, dec_seq, n_heads, HEAD_WIDTH))
        outs["hsr"].append(hsr.reshape(dec_b, n_groups, STATE_DIM))
        outs["hsi"].append(hsi.reshape(dec_b, n_groups, STATE_DIM))

    y_prompt = xp.reshape(bsz, t, d_model)[:, N_META:]
    y_sample = xs.reshape(dec_b, dec_seq, d_model)
    st = lambda name: jnp.stack(outs[name])
    return (y_prompt, y_sample, st("kp"), st("vp"), st("hpr"), st("hpi"), st("ks"), st("vs"),
            st("hsr"), st("hsi"))
```

```python
import functools
import math

import jax
import jax.numpy as jnp
from jax import lax
from jax.experimental import pallas as pl
from jax.experimental.pallas import tpu as pltpu

F32 = jnp.float32
BF16 = jnp.bfloat16

N_META = 16
HEAD_DIM = 64
HEAD_WIDTH = 2 * HEAD_DIM
GROUP_CH = 16
STATE_DIM = 64
PAGE_SIZE = 128
LANES = 128
N_BUCKETS = 32
MAX_DISTANCE = 128
EPS = 1e-5
LOG2E = math.log2(math.e)

ATT_Q_TILE = 256
SSM_CHUNK = 48
V7X_VMEM_LIMIT = 56 * 1024 * 1024

assert (N_BUCKETS // 2 + int(math.log((MAX_DISTANCE + 1) / (N_BUCKETS // 2))
                             / math.log(MAX_DISTANCE / (N_BUCKETS // 2)) * (N_BUCKETS // 2))
        >= N_BUCKETS - 1)


def _layer_norm(x, g, b):
    mu = jnp.mean(x, -1, keepdims=True)
    xc = x - mu
    var = jnp.mean(xc * xc, -1, keepdims=True)
    return xc * lax.rsqrt(var + EPS) * g + b


def _dot(a, b):
    return jnp.dot(a, b, preferred_element_type=F32)


def _dot_nt(a, b):
    return lax.dot_general(a, b, (((1,), (1,)), ((), ())), preferred_element_type=F32)


def _resident(shape):
    return pl.BlockSpec(shape, lambda *_: (0,) * len(shape), pipeline_mode=pl.Buffered(1))


def _row_tile(n_rows, target):
    best = None
    for t in range(16, min(n_rows, target) + 1, 16):
        if n_rows % t == 0:
            best = t
    assert best is not None, n_rows
    return best


def _in_proj_kernel(x_ref, g_ref, b_ref, w_ref, *out_refs, apply_ln, att_w, ssm_w):
    x = x_ref[...]
    if apply_ln:
        xo_ref, *out_refs = out_refs
        x = _layer_norm(x, g_ref[...], b_ref[...])
        xo_ref[...] = x
    q_ref, kf_ref, kb_ref, vf_ref, vb_ref, u_ref = out_refs
    xb = x.astype(BF16)
    c0 = 0

    def mm(width):
        nonlocal c0
        r = _dot(xb, w_ref[:, c0:c0 + width])
        c0 += width
        return r

    q_ref[...] = (mm(att_w) * (HEAD_DIM ** -0.5 * LOG2E)).astype(BF16)
    k = mm(att_w)
    kf_ref[...] = k
    kb_ref[...] = k.astype(BF16)
    v = mm(att_w)
    vf_ref[...] = v
    vb_ref[...] = v.astype(BF16)
    u_ref[...] = mm(ssm_w)


def _in_proj(x, ln_g, ln_b, w_qkvu, *, apply_ln, tm, att_w, ssm_w):
    n, d = x.shape
    row = lambda w: pl.BlockSpec((tm, w), lambda i: (i, 0))
    outs = [(att_w, BF16), (att_w, F32), (att_w, BF16), (att_w, F32), (att_w, BF16), (ssm_w, F32)]
    if apply_ln:
        outs = [(d, F32)] + outs
    return pl.pallas_call(
        functools.partial(_in_proj_kernel, apply_ln=apply_ln, att_w=att_w, ssm_w=ssm_w),
        grid=(n // tm,),
        in_specs=[row(d), _resident((1, d)), _resident((1, d)), _resident(w_qkvu.shape)],
        out_specs=[row(w) for w, _ in outs],
        out_shape=[jax.ShapeDtypeStruct((n, w), dt) for w, dt in outs],
        compiler_params=pltpu.CompilerParams(dimension_semantics=("parallel",),
                                             vmem_limit_bytes=V7X_VMEM_LIMIT),
        name="in_proj",
    )(x, ln_g, ln_b, w_qkvu)


def _bias_kernel(rb_ref, rel_ref, o_ref):
    rel = rel_ref[...]
    n = jnp.maximum(rel, 0)
    exact = N_BUCKETS // 2
    large = exact + (jnp.log(jnp.maximum(n, 1).astype(F32) / exact)
                     / math.log(MAX_DISTANCE / exact) * (N_BUCKETS - exact)).astype(jnp.int32)
    bucket = jnp.where(n < exact, n, jnp.minimum(large, N_BUCKETS - 1))
    for h in range(o_ref.shape[0]):
        far = rb_ref[N_BUCKETS - 1, h]
        acc = jnp.zeros(rel.shape, F32)
        for b in range(N_BUCKETS - 1):
            acc = jnp.where(bucket == b, (rb_ref[b, h] - far) * LOG2E, acc)
        o_ref[h] = jnp.where(rel >= 0, acc, -jnp.inf)


def _bias_tile(rel_bias, rel):
    n_heads = rel_bias.shape[1]
    return pl.pallas_call(
        _bias_kernel,
        in_specs=[pl.BlockSpec(memory_space=pltpu.SMEM), pl.BlockSpec(rel.shape, lambda: (0, 0))],
        out_specs=pl.BlockSpec((n_heads,) + rel.shape, lambda: (0, 0, 0)),
        out_shape=jax.ShapeDtypeStruct((n_heads,) + rel.shape, F32),
        name="bias_tile",
    )(rel_bias.astype(F32), rel)


def _bias_tables(rel_bias, tq, dec_seq):
    i32 = jnp.int32
    n_heads = rel_bias.shape[1]
    r = jnp.arange(tq, dtype=i32)[:, None]
    near = _bias_tile(rel_bias, tq + r - jnp.arange(2 * tq, dtype=i32)[None, :])
    mb = _bias_tile(rel_bias, N_META + r - jnp.arange(N_META, dtype=i32)[None, :])
    mr = jnp.arange(N_META, dtype=i32)
    mm = _bias_tile(rel_bias, mr[:, None] - mr[None, :])
    s = jnp.arange(dec_seq, dtype=i32)[:, None]
    kk = jnp.arange(PAGE_SIZE, dtype=i32)[None, :]
    last = _bias_tile(rel_bias, PAGE_SIZE + s - kk)
    new = _bias_tile(rel_bias, jnp.where(kk < dec_seq, s - kk, -1))

    def rows(tile):
        return jnp.broadcast_to(tile[:, None], (n_heads, 2, dec_seq, PAGE_SIZE)).reshape(-1, PAGE_SIZE)

    return near, mb, mm, jnp.stack([rows(last), rows(new)])


def _softmax_pv(blocks):
    m = functools.reduce(jnp.maximum, [jnp.max(s, -1, keepdims=True) for s, _ in blocks])
    acc = 0.0
    for s, v_ones in blocks:
        acc = acc + _dot(jnp.exp2(s - m).astype(BF16), v_ones)
    return acc[:, :HEAD_WIDTH] / acc[:, HEAD_WIDTH:HEAD_WIDTH + 1]


def _sub_norm(o1, o2, lam, g, sub_scale):
    o = o1 - lam * o2
    return o * lax.rsqrt(jnp.mean(o * o, -1, keepdims=True) + EPS) * g * sub_scale


def _attn_prompt_kernel(lam_ref, q_ref, k_ref, v_ref, near_ref, mb_ref, mm_ref, g_ref, o_ref,
                        k1_ref, k2_ref, v1_ref, *, tq, n_q, sub_scale):
    lam = lam_ref[0]
    g = g_ref[...]
    k = k_ref[0]
    lane = lax.broadcasted_iota(jnp.int32, k.shape, 1)
    zero = jnp.zeros_like(k)
    k1_ref[...] = jnp.where(lane < HEAD_DIM, k, zero)
    k2_ref[...] = jnp.where(lane >= HEAD_DIM, k, zero)
    v1_ref[:, :HEAD_WIDTH] = v_ref[0]
    v1_ref[:, HEAD_WIDTH:] = (lane == 0).astype(F32).astype(BF16)

    def attend(q_t, spans):
        outs = []
        for kc_ref in (k1_ref, k2_ref):
            blocks = []
            for k0, size, bias in spans:
                s = _dot_nt(q_t, kc_ref[k0:k0 + size, :])
                blocks.append((s if bias is None else s + bias, v1_ref[k0:k0 + size, :]))
            outs.append(_softmax_pv(blocks))
        return _sub_norm(outs[0], outs[1], lam, g, sub_scale).astype(BF16)

    o_ref[0, 0:N_META, :] = attend(q_ref[0, 0:N_META, :], [(0, N_META, mm_ref[0])])
    for i in range(n_q):
        r0 = N_META + i * tq
        q_t = q_ref[0, r0:r0 + tq, :]
        if i == 0:
            spans = [(0, N_META, mb_ref[0]), (r0, tq, near_ref[0, :, tq:])]
        else:
            spans = [(0, r0 - tq, None), (r0 - tq, 2 * tq, near_ref[0])]
        o_ref[0, r0:r0 + tq, :] = attend(q_t, spans)


def _attn_prompt(lam, q, kb, vb, near, mb, mm, g, *, sub_scale):
    bsz, t, att_w = q.shape
    n_heads = att_w // HEAD_WIDTH
    tq = near.shape[1]
    n_q = (t - N_META) // tq
    assert n_q * tq + N_META == t and tq > MAX_DISTANCE
    smem = pl.BlockSpec(memory_space=pltpu.SMEM)
    seq = pl.BlockSpec((1, t, HEAD_WIDTH), lambda b, h: (b, 0, h))
    per_head = lambda a: pl.BlockSpec((1,) + a.shape[1:], lambda b, h: (h, 0, 0))
    return pl.pallas_call(
        functools.partial(_attn_prompt_kernel, tq=tq, n_q=n_q, sub_scale=sub_scale),
        grid=(bsz, n_heads),
        in_specs=[smem, seq, seq, seq, per_head(near), per_head(mb), per_head(mm),
                  pl.BlockSpec((1, HEAD_WIDTH), lambda b, h: (0, 0))],
        out_specs=seq,
        out_shape=jax.ShapeDtypeStruct(q.shape, BF16),
        scratch_shapes=[pltpu.VMEM((t, HEAD_WIDTH), BF16), pltpu.VMEM((t, HEAD_WIDTH), BF16),
                        pltpu.VMEM((t, 2 * HEAD_WIDTH), BF16)],
        compiler_params=pltpu.CompilerParams(dimension_semantics=("parallel", "parallel"),
                                             vmem_limit_bytes=V7X_VMEM_LIMIT),
        name="attn_prompt",
    )(lam, q, kb, vb, near, mb, mm, g)


def _attn_sample_kernel(pt_ref, lam_ref, q_ref, kn_ref, vn_ref, bias_ref, g_ref, *rest,
                        n_pages, n_heads, dec_seq, sub_scale):
    kt_pages = rest[:n_pages]
    v_pages = rest[n_pages:2 * n_pages]
    o_ref, s_ref = rest[2 * n_pages:]
    del pt_ref
    lam = lam_ref[0]
    att_w = q_ref.shape[2]
    n_rows = 2 * n_heads * dec_seq
    head_rows = 2 * dec_seq
    q = q_ref[0]
    q_rep = jnp.concatenate([q] * (2 * n_heads), axis=0)
    row_blk = lax.broadcasted_iota(jnp.int32, (n_rows, att_w), 0) // dec_seq
    col_blk = lax.broadcasted_iota(jnp.int32, (n_rows, att_w), 1) // HEAD_DIM
    q_exp = jnp.where(row_blk == col_blk, q_rep, 0.0).astype(BF16)

    pad = jnp.zeros((PAGE_SIZE - dec_seq, att_w), F32)
    k_new = jnp.concatenate([kn_ref[0], pad], axis=0).astype(BF16)
    v_new = jnp.concatenate([vn_ref[0], pad], axis=0).astype(BF16)

    m_el = None
    for p in range(n_pages + 1):
        if p < n_pages:
            s = _dot(q_exp, kt_pages[p][...].astype(BF16))
            if p == n_pages - 1:
                s = s + bias_ref[0]
        else:
            s = _dot_nt(q_exp, k_new) + bias_ref[1]
        s_ref[p] = s
        m_el = s if m_el is None else jnp.maximum(m_el, s)
    m = jnp.max(m_el, -1, keepdims=True)

    l_el = jnp.zeros((n_rows, PAGE_SIZE), F32)
    accs = [jnp.zeros((head_rows, HEAD_WIDTH), F32) for _ in range(n_heads)]
    for p in range(n_pages + 1):
        e = jnp.exp2(s_ref[p] - m)
        l_el = l_el + e
        eb = e.astype(BF16)
        for h in range(n_heads):
            if p < n_pages:
                v_h = v_pages[p][pl.ds(h, PAGE_SIZE, stride=n_heads), :].astype(BF16)
            else:
                v_h = v_new[:, h * HEAD_WIDTH:(h + 1) * HEAD_WIDTH]
            accs[h] = accs[h] + _dot(eb[h * head_rows:(h + 1) * head_rows, :], v_h)
    l = jnp.sum(l_el, -1, keepdims=True)

    g = g_ref[...]
    heads = []
    for h in range(n_heads):
        o = accs[h] / l[h * head_rows:(h + 1) * head_rows]
        heads.append(_sub_norm(o[:dec_seq], o[dec_seq:], lam, g, sub_scale))
    o_ref[0] = jnp.concatenate(heads, axis=-1)


def _attn_sample(page_table, lam, q, k_new, v_new, bias, g, cache_kt, cache_v, *, layer, sub_scale):
    dec_b, dec_seq, att_w = q.shape
    n_pages = page_table.shape[1]
    n_heads = att_w // HEAD_WIDTH
    n_rows = 2 * n_heads * dec_seq
    pt_flat = page_table.reshape(-1)
    per_b = pl.BlockSpec((1, dec_seq, att_w), lambda b, pt: (b, 0, 0))

    def page_spec(p):
        return pl.BlockSpec((None, None) + cache_v.shape[2:],
                            lambda b, pt: (layer, pt[b * n_pages + p], 0, 0))

    pages = [page_spec(p) for p in range(n_pages)]
    grid_spec = pltpu.PrefetchScalarGridSpec(
        num_scalar_prefetch=1,
        grid=(dec_b,),
        in_specs=[pl.BlockSpec(memory_space=pltpu.SMEM), per_b, per_b, per_b,
                  pl.BlockSpec(bias.shape, lambda b, pt: (0, 0, 0)),
                  pl.BlockSpec((1, HEAD_WIDTH), lambda b, pt: (0, 0))] + pages + pages,
        out_specs=per_b,
        scratch_shapes=[pltpu.VMEM((n_pages + 1, n_rows, PAGE_SIZE), F32)],
    )
    return pl.pallas_call(
        functools.partial(_attn_sample_kernel, n_pages=n_pages, n_heads=n_heads, dec_seq=dec_seq,
                          sub_scale=sub_scale),
        grid_spec=grid_spec,
        out_shape=jax.ShapeDtypeStruct(q.shape, F32),
        compiler_params=pltpu.CompilerParams(dimension_semantics=("parallel",),
                                             vmem_limit_bytes=V7X_VMEM_LIMIT),
        name="attn_sample",
    )(pt_flat, lam, q, k_new, v_new, bias, g, *([cache_kt] * n_pages), *([cache_v] * n_pages))


def _ssm_kernel(u_ref, h0r_ref, h0i_ref, lbr_ref, lbi_ref, br_ref, bi_ref, cr_ref, ci_ref, d_ref,
                z_ref, hr_ref, hi_ref, u_tm, x_tm, z_tm, *, nb, tc):
    n_state = lbr_ref.shape[1]
    half_s = n_state // 2
    n_lane_chunks = u_tm.shape[0]
    half_c = n_lane_chunks // 2

    @pl.when(pl.program_id(0) == 0)
    def _():
        hr_ref[...] = h0r_ref[...]
        hi_ref[...] = h0i_ref[...]

    def to_time_major(b, c):
        ub = u_ref[b]
        for j in range(n_lane_chunks):
            u_tm[j, pl.ds(b, tc, stride=nb), :] = ub[:, j * LANES:(j + 1) * LANES]
        return c

    lax.fori_loop(0, nb, to_time_major, 0)

    def u_cols(j0, j1, rows=slice(None)):
        return jnp.concatenate([u_tm[j, rows, :] for j in range(j0, j1)], axis=-1)

    for half in range(2):
        uh = u_cols(half * half_c, (half + 1) * half_c).astype(BF16)
        x_tm[:, half * half_s:(half + 1) * half_s] = _dot(uh, br_ref[half])
        x_tm[:, n_state + half * half_s:n_state + (half + 1) * half_s] = _dot(uh, bi_ref[half])

    cw = 512
    for cg in range(n_state // cw):
        re = slice(cg * cw, (cg + 1) * cw)
        im = slice(n_state + cg * cw, n_state + (cg + 1) * cw)
        lr = lbr_ref[:, re]
        li = lbi_ref[:, re]

        def rows(rg, c, re=re, im=im, lr=lr, li=li):
            r0 = pl.multiple_of(rg * 8, 8)

            def step(t, h):
                hr, hi = h
                row = pl.multiple_of(t * nb + r0, 8)
                nr = lr * hr - li * hi + x_tm[pl.ds(row, 8), re]
                ni = lr * hi + li * hr + x_tm[pl.ds(row, 8), im]
                x_tm[pl.ds(row, 8), re] = nr
                x_tm[pl.ds(row, 8), im] = ni
                return nr, ni

            h = (hr_ref[pl.ds(r0, 8), re], hi_ref[pl.ds(r0, 8), re])
            hr, hi = lax.fori_loop(0, tc, step, h, unroll=8)
            hr_ref[pl.ds(r0, 8), re] = hr
            hi_ref[pl.ds(r0, 8), re] = hi
            return c

        lax.fori_loop(0, nb // 8, rows, 0)

    n_rows = nb * tc
    rc = n_rows if n_rows <= 512 else 256
    for r0 in range(0, n_rows, rc):
        hb = x_tm[r0:r0 + rc, :].astype(BF16)
        ys = []
        for half in range(2):
            y = _dot(hb[:, half * half_s:(half + 1) * half_s], cr_ref[half])
            y = y + _dot(hb[:, n_state + half * half_s:n_state + (half + 1) * half_s], ci_ref[half])
            ys.append(y)
        y = jnp.concatenate(ys, axis=-1) + d_ref[...] * u_cols(0, n_lane_chunks, slice(r0, r0 + rc))
        z = jax.nn.gelu(y)
        for j in range(n_lane_chunks):
            z_tm[j, r0:r0 + rc, :] = z[:, j * LANES:(j + 1) * LANES]

    def to_batch_major(b, c):
        z_ref[b] = jnp.concatenate(
            [z_tm[j, pl.ds(b, tc, stride=nb), :] for j in range(n_lane_chunks)], axis=-1)
        return c

    lax.fori_loop(0, nb, to_batch_major, 0)


def _ssm(u, h0_re, h0_im, lb_re, lb_im, b_re, b_im, c_re, c_im, d_vec, *, tc):
    nb, t, ssm_w = u.shape
    n_state = lb_re.shape[1]
    assert t % tc == 0 and nb % 8 == 0
    state = pl.BlockSpec((nb, n_state), lambda i: (0, 0))
    seq = pl.BlockSpec((nb, tc, ssm_w), lambda i: (0, i, 0))
    whole = lambda a: pl.BlockSpec(a.shape, lambda i: (0,) * a.ndim)
    return pl.pallas_call(
        functools.partial(_ssm_kernel, nb=nb, tc=tc),
        grid=(t // tc,),
        in_specs=[seq, state, state, whole(lb_re), whole(lb_im), whole(b_re), whole(b_im),
                  whole(c_re), whole(c_im), whole(d_vec)],
        out_specs=[seq, state, state],
        out_shape=[jax.ShapeDtypeStruct(u.shape, F32), jax.ShapeDtypeStruct((nb, n_state), F32),
                   jax.ShapeDtypeStruct((nb, n_state), F32)],
        scratch_shapes=[pltpu.VMEM((ssm_w // LANES, nb * tc, LANES), F32),
                        pltpu.VMEM((nb * tc, 2 * n_state), F32),
                        pltpu.VMEM((ssm_w // LANES, nb * tc, LANES), F32)],
        compiler_params=pltpu.CompilerParams(dimension_semantics=("arbitrary",),
                                             vmem_limit_bytes=V7X_VMEM_LIMIT),
        name="ssm",
    )(u, h0_re, h0_im, lb_re, lb_im, b_re, b_im, c_re, c_im, d_vec)


def _merge_kernel(x_ref, on_ref, z_ref, wgate_ref, wau_ref, wglu_ref, wout_ref, g_ref, b_ref,
                  o_ref, *, alpha):
    d = x_ref.shape[1]
    x = x_ref[...]
    gates = jax.nn.sigmoid(_dot(x.astype(BF16), wgate_ref[...]))
    a_branch = _dot(on_ref[...].astype(BF16), wau_ref[...])
    glu = _dot(z_ref[...].astype(BF16), wglu_ref[...])
    b_branch = glu[:, :d] * jax.nn.sigmoid(glu[:, d:])
    mix_in = (gates[:, :d] * a_branch + gates[:, d:] * b_branch).astype(BF16)
    mix = _dot(mix_in, wout_ref[...])
    o_ref[...] = _layer_norm(alpha * x + mix, g_ref[...], b_ref[...])


def _merge(x, o_n, z, w_gate, w_att_up, w_glu, w_out, ln_g, ln_b, *, tm, alpha):
    n, d = x.shape
    row = lambda a: pl.BlockSpec((tm, a.shape[1]), lambda i: (i, 0))
    return pl.pallas_call(
        functools.partial(_merge_kernel, alpha=alpha),
        grid=(n // tm,),
        in_specs=[row(x), row(o_n), row(z), _resident(w_gate.shape), _resident(w_att_up.shape),
                  _resident(w_glu.shape), _resident(w_out.shape), _resident((1, d)), _resident((1, d))],
        out_specs=row(x),
        out_shape=jax.ShapeDtypeStruct(x.shape, F32),
        compiler_params=pltpu.CompilerParams(dimension_semantics=("parallel",),
                                             vmem_limit_bytes=V7X_VMEM_LIMIT),
        name="merge",
    )(x, o_n, z, w_gate, w_att_up, w_glu, w_out, ln_g, ln_b)


def _ffn_kernel(x_ref, wup_ref, wdn_ref, g_ref, b_ref, o_ref, *, alpha, ff_chunk):
    x = x_ref[...]
    xb = x.astype(BF16)
    acc = alpha * x
    for c0 in range(0, wup_ref.shape[1], ff_chunk):
        hid = jnp.maximum(_dot(xb, wup_ref[:, c0:c0 + ff_chunk]), 0.0)
        acc = acc + _dot((hid * hid).astype(BF16), wdn_ref[c0:c0 + ff_chunk, :])
    o_ref[...] = _layer_norm(acc, g_ref[...], b_ref[...])


def _ffn(x, w_up, w_down, ln_g, ln_b, *, tm, alpha):
    n, d = x.shape
    row = pl.BlockSpec((tm, d), lambda i: (i, 0))
    return pl.pallas_call(
        functools.partial(_ffn_kernel, alpha=alpha, ff_chunk=d),
        grid=(n // tm,),
        in_specs=[row, _resident(w_up.shape), _resident(w_down.shape), _resident((1, d)),
                  _resident((1, d))],
        out_specs=row,
        out_shape=jax.ShapeDtypeStruct(x.shape, F32),
        compiler_params=pltpu.CompilerParams(dimension_semantics=("parallel",),
                                             vmem_limit_bytes=V7X_VMEM_LIMIT),
        name="ffn",
    )(x, w_up, w_down, ln_g, ln_b)


def _ffn_skip_meta(x, w_up, w_down, ln_g, ln_b, *, tm, alpha):
    bsz, t, d = x.shape
    seq = t - N_META
    assert seq % tm == 0
    window = pl.BlockSpec((pl.Squeezed(), pl.Element(tm), pl.Element(d)),
                          lambda b, j: (b, pl.multiple_of(N_META + j * tm, N_META), 0))
    return pl.pallas_call(
        functools.partial(_ffn_kernel, alpha=alpha, ff_chunk=d),
        grid=(bsz, seq // tm),
        in_specs=[window, _resident(w_up.shape), _resident(w_down.shape), _resident((1, d)),
                  _resident((1, d))],
        out_specs=pl.BlockSpec((None, tm, d), lambda b, j: (b, j, 0)),
        out_shape=jax.ShapeDtypeStruct((bsz, seq, d), F32),
        compiler_params=pltpu.CompilerParams(dimension_semantics=("parallel", "parallel"),
                                             vmem_limit_bytes=V7X_VMEM_LIMIT),
        name="ffn_skip_meta",
    )(x, w_up, w_down, ln_g, ln_b)


def _ssm_tables(a_re, a_im, log_dt, b_re, b_im, c_re, c_im, d):
    n_groups, n_p = a_re.shape
    a_re = a_re.astype(F32)
    a_im = a_im.astype(F32)
    dt = jnp.exp(log_dt.astype(F32))[:, None]
    mag = jnp.exp(a_re * dt)
    lb_re = mag * jnp.cos(a_im * dt)
    lb_im = mag * jnp.sin(a_im * dt)
    nr, ni = lb_re - 1.0, lb_im
    den = a_re * a_re + a_im * a_im
    f_re = (nr * a_re + ni * a_im) / den
    f_im = (ni * a_re - nr * a_im) / den
    b_re = b_re.astype(F32)
    b_im = b_im.astype(F32)
    bb_re = f_re[..., None] * b_re - f_im[..., None] * b_im
    bb_im = f_re[..., None] * b_im + f_im[..., None] * b_re
    eye = jnp.eye(n_groups, dtype=F32)
    hg = n_groups // 2

    def in_mat(bb):
        full = jnp.einsum('gpc,gh->gchp', bb, eye).reshape(n_groups * GROUP_CH, n_groups * n_p)
        hu, hs = hg * GROUP_CH, hg * n_p
        return jnp.stack([full[:hu, :hs], full[hu:, hs:]]).astype(BF16)

    def out_mat(cc):
        full = jnp.einsum('gcp,gh->gphc', cc, eye).reshape(n_groups * n_p, n_groups * GROUP_CH)
        hu, hs = hg * GROUP_CH, hg * n_p
        return jnp.stack([full[:hs, :hu], full[hs:, hu:]]).astype(BF16)

    lb = (jnp.broadcast_to(lb_re.reshape(1, -1), (8, n_groups * n_p)),
          jnp.broadcast_to(lb_im.reshape(1, -1), (8, n_groups * n_p)))
    return (lb, in_mat(bb_re), in_mat(bb_im), out_mat(c_re.astype(F32)), out_mat(-c_im.astype(F32)),
            d.astype(F32).reshape(1, -1))


def kernel(x_prompt, x_sample, cache_k, cache_v, state_ssm_re, state_ssm_im, page_table, meta_tokens, ln_in_g, ln_in_b, rel_bias, w_in, lambda_q1, lambda_k1, lambda_q2, lambda_k2, subln_g, w_att_up, ssm_a_re, ssm_a_im, ssm_log_dt, ssm_b_re, ssm_b_im, ssm_c_re, ssm_c_im, ssm_d, w_glu, w_out, ln1_g, ln1_b, w_up, w_down, ln2_g, ln2_b):
    bsz, seq, d_model = x_prompt.shape
    dec_b, dec_seq, _ = x_sample.shape
    depth = w_in.shape[0]
    att_w = w_att_up.shape[1]
    ssm_w = w_glu.shape[1]
    n_heads = att_w // HEAD_WIDTH
    n_groups = ssm_w // GROUP_CH
    n_state = n_groups * STATE_DIM
    t = N_META + seq
    past = page_table.shape[1] * PAGE_SIZE
    alpha = (2.0 * depth) ** 0.25
    assert seq % ATT_Q_TILE == 0 and t % SSM_CHUNK == 0 and past >= 2 * PAGE_SIZE

    tm_p = _row_tile(t, 704)
    tm_s = _row_tile(dec_b * dec_seq, 512)

    meta = jnp.broadcast_to(meta_tokens[None].astype(x_prompt.dtype), (bsz, N_META, d_model))
    xp = jnp.concatenate([meta, x_prompt], axis=1).reshape(bsz * t, d_model)
    xs = x_sample.reshape(dec_b * dec_seq, d_model)
    row = lambda a: a.astype(F32).reshape(1, -1)
    near, mb, mm, bias_s = _bias_tables(rel_bias, ATT_Q_TILE, dec_seq)
    n_pool = cache_k.shape[1]
    cache_kt = jnp.transpose(cache_k, (0, 1, 3, 4, 5, 2)).reshape(depth, n_pool, att_w, PAGE_SIZE)
    cache_v = cache_v.reshape(depth, n_pool, PAGE_SIZE * n_heads, HEAD_WIDTH)
    h0p = jnp.zeros((bsz, n_state), F32)

    outs = {name: [] for name in ("kp", "vp", "hpr", "hpi", "ks", "vs", "hsr", "hsi")}
    for l in range(depth):
        lam_init = 0.8 - 0.6 * math.exp(-0.3 * l)
        lam = (jnp.exp(jnp.sum(lambda_q1[l].astype(F32) * lambda_k1[l].astype(F32)))
               - jnp.exp(jnp.sum(lambda_q2[l].astype(F32) * lambda_k2[l].astype(F32))) + lam_init)
        lam = lam.reshape(1)
        sub_scale = 1.0 - lam_init
        g_sub = row(subln_g[l])
        n_qkvu = 3 * att_w + ssm_w
        w_qkvu_bf = w_in[l, :, :n_qkvu].astype(BF16)
        w_gate_bf = w_in[l, :, n_qkvu:].astype(BF16)
        w_au_bf = w_att_up[l].astype(BF16)
        w_glu_bf = w_glu[l].astype(BF16)
        w_out_bf = w_out[l].astype(BF16)
        w_up_bf = w_up[l].astype(BF16)
        w_dn_bf = w_down[l].astype(BF16)
        lb, sb_re, sb_im, sc_re, sc_im, d_vec = _ssm_tables(
            ssm_a_re[l], ssm_a_im[l], ssm_log_dt[l], ssm_b_re[l], ssm_b_im[l], ssm_c_re[l],
            ssm_c_im[l], ssm_d[l])
        ln = dict(apply_ln=(l == 0), att_w=att_w, ssm_w=ssm_w)

        *x_ln, q, kf, kb, vf, vb, u = _in_proj(xp, row(ln_in_g), row(ln_in_b), w_qkvu_bf,
                                               tm=tm_p, **ln)
        xp = x_ln[0] if x_ln else xp
        seq3 = lambda a: a.reshape(bsz, t, a.shape[-1])
        o_n = _attn_prompt(lam, seq3(q), seq3(kb), seq3(vb), near, mb, mm, g_sub,
                           sub_scale=sub_scale)
        z, hpr, hpi = _ssm(seq3(u), h0p, h0p, *lb, sb_re, sb_im, sc_re, sc_im, d_vec,
                           tc=SSM_CHUNK)
        xp = _merge(xp, o_n.reshape(bsz * t, att_w), z.reshape(bsz * t, ssm_w), w_gate_bf, w_au_bf,
                    w_glu_bf, w_out_bf, row(ln1_g[l]), row(ln1_b[l]), tm=tm_p, alpha=alpha)
        if l < depth - 1:
            xp = _ffn(xp, w_up_bf, w_dn_bf, row(ln2_g[l]), row(ln2_b[l]), tm=tm_p, alpha=alpha)
        else:
            y_prompt = _ffn_skip_meta(xp.reshape(bsz, t, d_model), w_up_bf, w_dn_bf, row(ln2_g[l]),
                                      row(ln2_b[l]), tm=_row_tile(seq, 512), alpha=alpha)
        outs["kp"].append(kf.reshape(bsz, t, n_heads, 2, HEAD_DIM))
        outs["vp"].append(vf.reshape(bsz, t, n_heads, HEAD_WIDTH))
        outs["hpr"].append(hpr.reshape(bsz, n_groups, STATE_DIM))
        outs["hpi"].append(hpi.reshape(bsz, n_groups, STATE_DIM))

        *x_ln, q, kf, kb, vf, vb, u = _in_proj(xs, row(ln_in_g), row(ln_in_b), w_qkvu_bf,
                                               tm=tm_s, **ln)
        xs = x_ln[0] if x_ln else xs
        dec3 = lambda a: a.reshape(dec_b, dec_seq, a.shape[-1])
        o_n = _attn_sample(page_table, lam, dec3(q.astype(F32)), dec3(kf), dec3(vf), bias_s, g_sub,
                           cache_kt, cache_v, layer=l, sub_scale=sub_scale)
        z, hsr, hsi = _ssm(dec3(u), state_ssm_re[l].reshape(dec_b, n_state).astype(F32),
                           state_ssm_im[l].reshape(dec_b, n_state).astype(F32), *lb, sb_re,
                           sb_im, sc_re, sc_im, d_vec, tc=dec_seq)
        xs = _merge(xs, o_n.reshape(dec_b * dec_seq, att_w), z.reshape(dec_b * dec_seq, ssm_w),
                    w_gate_bf, w_au_bf, w_glu_bf, w_out_bf, row(ln1_g[l]), row(ln1_b[l]), tm=tm_s,
                    alpha=alpha)
        xs = _ffn(xs, w_up_bf, w_dn_bf, row(ln2_g[l]), row(ln2_b[l]), tm=tm_s, alpha=alpha)
        outs["ks"].append(kf.reshape(dec_b, dec_seq, n_heads, 2, HEAD_DIM))
        outs["vs"].append(vf.reshape(dec_b, dec_seq, n_heads, HEAD_WIDTH))
        outs["hsr"].append(hsr.reshape(dec_b, n_groups, STATE_DIM))
        outs["hsi"].append(hsi.reshape(dec_b, n_groups, STATE_DIM))

    y_sample = xs.reshape(dec_b, dec_seq, d_model)
    st = lambda name: jnp.stack(outs[name])
    return (y_prompt, y_sample, st("kp"), st("vp"), st("hpr"), st("hpi"), st("ks"), st("vs"),
            st("hsr"), st("hsi"))
```

```python
import functools
import math

import jax
import jax.numpy as jnp
from jax import lax
from jax.experimental import pallas as pl
from jax.experimental.pallas import tpu as pltpu

F32 = jnp.float32
BF16 = jnp.bfloat16

N_META = 16
HEAD_DIM = 64
HEAD_WIDTH = 2 * HEAD_DIM
GROUP_CH = 16
STATE_DIM = 64
PAGE_SIZE = 128
LANES = 128
N_BUCKETS = 32
MAX_DISTANCE = 128
EPS = 1e-5
LOG2E = math.log2(math.e)

ATT_Q_TILE = 256
SSM_CHUNK = 48
V7X_VMEM_LIMIT = 56 * 1024 * 1024

assert (N_BUCKETS // 2 + int(math.log((MAX_DISTANCE + 1) / (N_BUCKETS // 2))
                             / math.log(MAX_DISTANCE / (N_BUCKETS // 2)) * (N_BUCKETS // 2))
        >= N_BUCKETS - 1)


def _layer_norm(x, g, b):
    mu = jnp.mean(x, -1, keepdims=True)
    xc = x - mu
    var = jnp.mean(xc * xc, -1, keepdims=True)
    return xc * lax.rsqrt(var + EPS) * g + b


def _dot(a, b):
    return jnp.dot(a, b, preferred_element_type=F32)


def _dot_nt(a, b):
    return lax.dot_general(a, b, (((1,), (1,)), ((), ())), preferred_element_type=F32)


def _resident(shape):
    return pl.BlockSpec(shape, lambda *_: (0,) * len(shape), pipeline_mode=pl.Buffered(1))


def _row_tile(n_rows, target):
    best = None
    for t in range(16, min(n_rows, target) + 1, 16):
        if n_rows % t == 0:
            best = t
    assert best is not None, n_rows
    return best


def _project_qkvu(x, w_ref, att_w, ssm_w):
    xb = x.astype(BF16)
    widths = (att_w, att_w, att_w, ssm_w)
    starts = [sum(widths[:i]) for i in range(len(widths))]
    q, k, v, u = (_dot(xb, w_ref[:, c0:c0 + w]) for c0, w in zip(starts, widths))
    return (q * (HEAD_DIM ** -0.5 * LOG2E)).astype(BF16), k, v, u


def _in_proj_kernel(x_ref, g_ref, b_ref, w_ref, *out_refs, apply_ln, att_w, ssm_w):
    x = x_ref[...]
    if apply_ln:
        xo_ref, *out_refs = out_refs
        x = _layer_norm(x, g_ref[...], b_ref[...])
        xo_ref[...] = x
    q_ref, k_ref, v_ref, u_ref = out_refs
    q_ref[...], k_ref[...], v_ref[...], u_ref[...] = _project_qkvu(x, w_ref, att_w, ssm_w)


def _in_proj(x, ln_g, ln_b, w_qkvu, *, apply_ln, tm, att_w, ssm_w):
    n, d = x.shape
    row = lambda w: pl.BlockSpec((tm, w), lambda i: (i, 0))
    outs = [(att_w, BF16), (att_w, F32), (att_w, F32), (ssm_w, F32)]
    if apply_ln:
        outs = [(d, F32)] + outs
    return pl.pallas_call(
        functools.partial(_in_proj_kernel, apply_ln=apply_ln, att_w=att_w, ssm_w=ssm_w),
        grid=(n // tm,),
        in_specs=[row(d), _resident((1, d)), _resident((1, d)), _resident(w_qkvu.shape)],
        out_specs=[row(w) for w, _ in outs],
        out_shape=[jax.ShapeDtypeStruct((n, w), dt) for w, dt in outs],
        compiler_params=pltpu.CompilerParams(dimension_semantics=("parallel",),
                                             vmem_limit_bytes=V7X_VMEM_LIMIT),
        name="in_proj",
    )(x, ln_g, ln_b, w_qkvu)


def _in_proj_prompt_kernel(*refs, first, tm, att_w, ssm_w):
    if first:
        x_ref, meta_ref, g_ref, b_ref, w_ref, xo_ref, q_ref, k_ref, vb_ref, u_ref, vall_ref = refs
    else:
        x_ref, w_ref, _, q_ref, k_ref, vb_ref, u_ref, vall_ref = refs
    n_heads = att_w // HEAD_WIDTH

    def body(x):
        if first:
            x = _layer_norm(x, g_ref[...], b_ref[...])
            xo_ref[...] = x
        q_ref[...], k_ref[...], v, u_ref[...] = _project_qkvu(x, w_ref, att_w, ssm_w)
        vb_ref[...] = v.astype(BF16)
        for h in range(n_heads):
            vall_ref[pl.ds(h, tm, stride=n_heads), :] = v[:, h * HEAD_WIDTH:(h + 1) * HEAD_WIDTH]

    if first:
        @pl.when(pl.program_id(1) == 0)
        def _():
            body(jnp.concatenate([meta_ref[...], x_ref[:tm - N_META, :]], axis=0))

        @pl.when(pl.program_id(1) != 0)
        def _():
            body(x_ref[...])
    else:
        body(x_ref[...])


def _in_proj_prompt(x, meta, ln_g, ln_b, w_qkvu, v_all, *, layer, depth, bsz, t, tm, att_w, ssm_w):
    first = layer == 0
    d = x.shape[-1]
    n = bsz * t
    n_j = t // tm
    n_heads = att_w // HEAD_WIDTH
    row = lambda w: pl.BlockSpec((tm, w), lambda b, j: (b * n_j + j, 0))
    outs = [(att_w, BF16), (att_w, F32), (att_w, BF16), (ssm_w, F32)]
    if first:
        window = pl.BlockSpec(
            (pl.Squeezed(), pl.Element(tm), pl.Element(d)),
            lambda b, j: (b, pl.multiple_of(jnp.maximum(j * tm - N_META, 0), N_META), 0))
        in_specs = [window, _resident(meta.shape), _resident((1, d)), _resident((1, d)),
                    _resident(w_qkvu.shape)]
        args = (x, meta, ln_g, ln_b, w_qkvu)
        outs = [(d, F32)] + outs
        aliases = {}
    else:
        in_specs = [row(d), _resident(w_qkvu.shape), pl.BlockSpec(memory_space=pl.ANY)]
        args = (x, w_qkvu, v_all)
        aliases = {2: len(outs)}
    vall_shape = (depth, n * n_heads, HEAD_WIDTH)
    return pl.pallas_call(
        functools.partial(_in_proj_prompt_kernel, first=first, tm=tm, att_w=att_w, ssm_w=ssm_w),
        grid=(bsz, n_j),
        in_specs=in_specs,
        out_specs=[row(w) for w, _ in outs]
        + [pl.BlockSpec((None, tm * n_heads, HEAD_WIDTH), lambda b, j: (layer, b * n_j + j, 0))],
        out_shape=[jax.ShapeDtypeStruct((n, w), dt) for w, dt in outs]
        + [jax.ShapeDtypeStruct(vall_shape, F32)],
        input_output_aliases=aliases,
        compiler_params=pltpu.CompilerParams(dimension_semantics=("parallel", "parallel"),
                                             vmem_limit_bytes=V7X_VMEM_LIMIT),
        name="in_proj_prompt",
    )(*args)


def _bias_kernel(rb_ref, rel_ref, o_ref):
    rel = rel_ref[...]
    n = jnp.maximum(rel, 0)
    exact = N_BUCKETS // 2
    large = exact + (jnp.log(jnp.maximum(n, 1).astype(F32) / exact)
                     / math.log(MAX_DISTANCE / exact) * (N_BUCKETS - exact)).astype(jnp.int32)
    bucket = jnp.where(n < exact, n, jnp.minimum(large, N_BUCKETS - 1))
    for h in range(o_ref.shape[0]):
        far = rb_ref[N_BUCKETS - 1, h]
        acc = jnp.zeros(rel.shape, F32)
        for b in range(N_BUCKETS - 1):
            acc = jnp.where(bucket == b, (rb_ref[b, h] - far) * LOG2E, acc)
        o_ref[h] = jnp.where(rel >= 0, acc, -jnp.inf)


def _bias_tile(rel_bias, rel):
    n_heads = rel_bias.shape[1]
    return pl.pallas_call(
        _bias_kernel,
        in_specs=[pl.BlockSpec(memory_space=pltpu.SMEM), pl.BlockSpec(rel.shape, lambda: (0, 0))],
        out_specs=pl.BlockSpec((n_heads,) + rel.shape, lambda: (0, 0, 0)),
        out_shape=jax.ShapeDtypeStruct((n_heads,) + rel.shape, F32),
        name="bias_tile",
    )(rel_bias.astype(F32), rel)


def _bias_tables(rel_bias, tq, dec_seq):
    i32 = jnp.int32
    n_heads = rel_bias.shape[1]
    r = jnp.arange(tq, dtype=i32)[:, None]
    near = _bias_tile(rel_bias, tq + r - jnp.arange(2 * tq, dtype=i32)[None, :])
    mb = _bias_tile(rel_bias, N_META + r - jnp.arange(N_META, dtype=i32)[None, :])
    mr = jnp.arange(N_META, dtype=i32)
    mm = _bias_tile(rel_bias, mr[:, None] - mr[None, :])
    s = jnp.arange(dec_seq, dtype=i32)[:, None]
    kk = jnp.arange(PAGE_SIZE, dtype=i32)[None, :]
    last = _bias_tile(rel_bias, PAGE_SIZE + s - kk)
    new = _bias_tile(rel_bias, jnp.where(kk < dec_seq, s - kk, -1))

    def rows(tile):
        return jnp.broadcast_to(tile[:, None], (n_heads, 2, dec_seq, PAGE_SIZE)).reshape(-1, PAGE_SIZE)

    return near, mb, mm, jnp.stack([rows(last), rows(new)])


def _softmax_pv(blocks):
    m = functools.reduce(jnp.maximum, [jnp.max(s, -1, keepdims=True) for s, _ in blocks])
    acc = 0.0
    for s, v_ones in blocks:
        acc = acc + _dot(jnp.exp2(s - m).astype(BF16), v_ones)
    return acc[:, :HEAD_WIDTH] / acc[:, HEAD_WIDTH:HEAD_WIDTH + 1]


def _sub_norm(o1, o2, lam, g, sub_scale):
    o = o1 - lam * o2
    return o * lax.rsqrt(jnp.mean(o * o, -1, keepdims=True) + EPS) * g * sub_scale


def _attn_prompt_kernel(lam_ref, q_ref, k_ref, v_ref, near_ref, mb_ref, mm_ref, g_ref, *rest,
                        tq, n_q, sub_scale):
    o_ref, kt_ref, k1_ref, k2_ref, v1_ref = rest[-5:]
    lam = lam_ref[0]
    g = g_ref[...]
    k = k_ref[0]
    t = k.shape[0]
    lane = lax.broadcasted_iota(jnp.int32, k.shape, 1)
    k1_ref[...] = jnp.where(lane < HEAD_DIM, k, 0.0).astype(BF16)
    k2_ref[...] = jnp.where(lane >= HEAD_DIM, k, 0.0).astype(BF16)
    v1_ref[:, :HEAD_WIDTH] = v_ref[0]
    v1_ref[:, HEAD_WIDTH:] = (lane == 0).astype(F32).astype(BF16)

    n_full = t // LANES
    for c in range(n_full):
        kt_ref[:, c * LANES:(c + 1) * LANES] = k_ref[0, c * LANES:(c + 1) * LANES, :].T
    tail = t - n_full * LANES
    if tail:
        k_tail = jnp.concatenate([k_ref[0, n_full * LANES:, :],
                                  jnp.zeros((LANES - tail, HEAD_WIDTH), F32)], axis=0)
        kt_ref[:, n_full * LANES:] = k_tail.T[:, :tail]

    def attend(q_t, spans):
        outs = []
        for kc_ref in (k1_ref, k2_ref):
            blocks = []
            for k0, size, bias in spans:
                s = _dot_nt(q_t, kc_ref[k0:k0 + size, :])
                blocks.append((s if bias is None else s + bias, v1_ref[k0:k0 + size, :]))
            outs.append(_softmax_pv(blocks))
        return _sub_norm(outs[0], outs[1], lam, g, sub_scale).astype(BF16)

    o_ref[0, 0:N_META, :] = attend(q_ref[0, 0:N_META, :], [(0, N_META, mm_ref[0])])
    for i in range(n_q):
        r0 = N_META + i * tq
        q_t = q_ref[0, r0:r0 + tq, :]
        if i == 0:
            spans = [(0, N_META, mb_ref[0]), (r0, tq, near_ref[0, :, tq:])]
        else:
            spans = [(0, r0 - tq, None), (r0 - tq, 2 * tq, near_ref[0])]
        o_ref[0, r0:r0 + tq, :] = attend(q_t, spans)


def _attn_prompt(lam, q, k, vb, near, mb, mm, g, kt_all, *, layer, depth, sub_scale):
    bsz, t, att_w = q.shape
    n_heads = att_w // HEAD_WIDTH
    tq = near.shape[1]
    n_q = (t - N_META) // tq
    assert n_q * tq + N_META == t and tq > MAX_DISTANCE
    smem = pl.BlockSpec(memory_space=pltpu.SMEM)
    seq = pl.BlockSpec((1, t, HEAD_WIDTH), lambda b, h: (b, 0, h))
    per_head = lambda a: pl.BlockSpec((1,) + a.shape[1:], lambda b, h: (h, 0, 0))
    in_specs = [smem, seq, seq, seq, per_head(near), per_head(mb), per_head(mm),
                pl.BlockSpec((1, HEAD_WIDTH), lambda b, h: (0, 0))]
    args = (lam, q, k, vb, near, mb, mm, g)
    aliases = {}
    if kt_all is not None:
        in_specs.append(pl.BlockSpec(memory_space=pl.ANY))
        args += (kt_all,)
        aliases = {len(args) - 1: 1}
    return pl.pallas_call(
        functools.partial(_attn_prompt_kernel, tq=tq, n_q=n_q, sub_scale=sub_scale),
        grid=(bsz, n_heads),
        in_specs=in_specs,
        out_specs=[seq, pl.BlockSpec((None, None, HEAD_WIDTH, t), lambda b, h: (layer, b, h, 0))],
        out_shape=[jax.ShapeDtypeStruct(q.shape, BF16),
                   jax.ShapeDtypeStruct((depth, bsz, att_w, t), F32)],
        input_output_aliases=aliases,
        scratch_shapes=[pltpu.VMEM((t, HEAD_WIDTH), BF16), pltpu.VMEM((t, HEAD_WIDTH), BF16),
                        pltpu.VMEM((t, 2 * HEAD_WIDTH), BF16)],
        compiler_params=pltpu.CompilerParams(dimension_semantics=("parallel", "parallel"),
                                             vmem_limit_bytes=V7X_VMEM_LIMIT),
        name="attn_prompt",
    )(*args)


def _attn_sample_kernel(pt_ref, lam_ref, q_ref, kn_ref, vn_ref, bias_ref, g_ref, *rest,
                        n_pages, n_heads, dec_seq, sub_scale):
    kt_pages = rest[:n_pages]
    v_pages = rest[n_pages:2 * n_pages]
    o_ref, s_ref = rest[2 * n_pages:]
    del pt_ref
    lam = lam_ref[0]
    att_w = q_ref.shape[2]
    n_rows = 2 * n_heads * dec_seq
    head_rows = 2 * dec_seq
    q = q_ref[0]
    q_rep = jnp.concatenate([q] * (2 * n_heads), axis=0)
    row_blk = lax.broadcasted_iota(jnp.int32, (n_rows, att_w), 0) // dec_seq
    col_blk = lax.broadcasted_iota(jnp.int32, (n_rows, att_w), 1) // HEAD_DIM
    q_exp = jnp.where(row_blk == col_blk, q_rep, 0.0).astype(BF16)

    pad = jnp.zeros((PAGE_SIZE - dec_seq, att_w), F32)
    k_new = jnp.concatenate([kn_ref[0], pad], axis=0).astype(BF16)
    v_new = jnp.concatenate([vn_ref[0], pad], axis=0).astype(BF16)

    m_el = None
    for p in range(n_pages + 1):
        if p < n_pages:
            s = _dot(q_exp, kt_pages[p][...].astype(BF16))
            if p == n_pages - 1:
                s = s + bias_ref[0]
        else:
            s = _dot_nt(q_exp, k_new) + bias_ref[1]
        s_ref[p] = s
        m_el = s if m_el is None else jnp.maximum(m_el, s)
    m = jnp.max(m_el, -1, keepdims=True)

    l_el = jnp.zeros((n_rows, PAGE_SIZE), F32)
    accs = [jnp.zeros((head_rows, HEAD_WIDTH), F32) for _ in range(n_heads)]
    for p in range(n_pages + 1):
        e = jnp.exp2(s_ref[p] - m)
        l_el = l_el + e
        eb = e.astype(BF16)
        for h in range(n_heads):
            if p < n_pages:
                v_h = v_pages[p][pl.ds(h, PAGE_SIZE, stride=n_heads), :].astype(BF16)
            else:
                v_h = v_new[:, h * HEAD_WIDTH:(h + 1) * HEAD_WIDTH]
            accs[h] = accs[h] + _dot(eb[h * head_rows:(h + 1) * head_rows, :], v_h)
    l = jnp.sum(l_el, -1, keepdims=True)

    g = g_ref[...]
    heads = []
    for h in range(n_heads):
        o = accs[h] / l[h * head_rows:(h + 1) * head_rows]
        heads.append(_sub_norm(o[:dec_seq], o[dec_seq:], lam, g, sub_scale))
    o_ref[0] = jnp.concatenate(heads, axis=-1)


def _attn_sample(page_table, lam, q, k_new, v_new, bias, g, cache_kt, cache_v, *, layer, sub_scale):
    dec_b, dec_seq, att_w = q.shape
    n_pages = page_table.shape[1]
    n_heads = att_w // HEAD_WIDTH
    n_rows = 2 * n_heads * dec_seq
    pt_flat = page_table.reshape(-1)
    per_b = pl.BlockSpec((1, dec_seq, att_w), lambda b, pt: (b, 0, 0))

    def page_spec(p):
        return pl.BlockSpec((None, None) + cache_v.shape[2:],
                            lambda b, pt: (layer, pt[b * n_pages + p], 0, 0))

    pages = [page_spec(p) for p in range(n_pages)]
    grid_spec = pltpu.PrefetchScalarGridSpec(
        num_scalar_prefetch=1,
        grid=(dec_b,),
        in_specs=[pl.BlockSpec(memory_space=pltpu.SMEM), per_b, per_b, per_b,
                  pl.BlockSpec(bias.shape, lambda b, pt: (0, 0, 0)),
                  pl.BlockSpec((1, HEAD_WIDTH), lambda b, pt: (0, 0))] + pages + pages,
        out_specs=per_b,
        scratch_shapes=[pltpu.VMEM((n_pages + 1, n_rows, PAGE_SIZE), F32)],
    )
    return pl.pallas_call(
        functools.partial(_attn_sample_kernel, n_pages=n_pages, n_heads=n_heads, dec_seq=dec_seq,
                          sub_scale=sub_scale),
        grid_spec=grid_spec,
        out_shape=jax.ShapeDtypeStruct(q.shape, F32),
        compiler_params=pltpu.CompilerParams(dimension_semantics=("parallel",),
                                             vmem_limit_bytes=V7X_VMEM_LIMIT),
        name="attn_sample",
    )(pt_flat, lam, q, k_new, v_new, bias, g, *([cache_kt] * n_pages), *([cache_v] * n_pages))


def _ssm_kernel(u_ref, h0r_ref, h0i_ref, lbr_ref, lbi_ref, br_ref, bi_ref, cr_ref, ci_ref, d_ref,
                z_ref, hr_ref, hi_ref, u_tm, x_tm, z_tm, *, nb, tc):
    n_state = lbr_ref.shape[1]
    half_s = n_state // 2
    n_lane_chunks = u_tm.shape[0]
    half_c = n_lane_chunks // 2

    @pl.when(pl.program_id(0) == 0)
    def _():
        hr_ref[...] = h0r_ref[...]
        hi_ref[...] = h0i_ref[...]

    def to_time_major(b, c):
        ub = u_ref[b]
        for j in range(n_lane_chunks):
            u_tm[j, pl.ds(b, tc, stride=nb), :] = ub[:, j * LANES:(j + 1) * LANES]
        return c

    lax.fori_loop(0, nb, to_time_major, 0)

    def u_cols(j0, j1, rows=slice(None)):
        return jnp.concatenate([u_tm[j, rows, :] for j in range(j0, j1)], axis=-1)

    for half in range(2):
        uh = u_cols(half * half_c, (half + 1) * half_c).astype(BF16)
        x_tm[:, half * half_s:(half + 1) * half_s] = _dot(uh, br_ref[half])
        x_tm[:, n_state + half * half_s:n_state + (half + 1) * half_s] = _dot(uh, bi_ref[half])

    cw = 512
    for cg in range(n_state // cw):
        re = slice(cg * cw, (cg + 1) * cw)
        im = slice(n_state + cg * cw, n_state + (cg + 1) * cw)
        lr = lbr_ref[:, re]
        li = lbi_ref[:, re]

        def rows(rg, c, re=re, im=im, lr=lr, li=li):
            r0 = pl.multiple_of(rg * 8, 8)

            def step(t, h):
                hr, hi = h
                row = pl.multiple_of(t * nb + r0, 8)
                nr = lr * hr - li * hi + x_tm[pl.ds(row, 8), re]
                ni = lr * hi + li * hr + x_tm[pl.ds(row, 8), im]
                x_tm[pl.ds(row, 8), re] = nr
                x_tm[pl.ds(row, 8), im] = ni
                return nr, ni

            h = (hr_ref[pl.ds(r0, 8), re], hi_ref[pl.ds(r0, 8), re])
            hr, hi = lax.fori_loop(0, tc, step, h, unroll=8)
            hr_ref[pl.ds(r0, 8), re] = hr
            hi_ref[pl.ds(r0, 8), re] = hi
            return c

        lax.fori_loop(0, nb // 8, rows, 0)

    n_rows = nb * tc
    rc = n_rows if n_rows <= 512 else 256
    for r0 in range(0, n_rows, rc):
        hb = x_tm[r0:r0 + rc, :].astype(BF16)
        ys = []
        for half in range(2):
            y = _dot(hb[:, half * half_s:(half + 1) * half_s], cr_ref[half])
            y = y + _dot(hb[:, n_state + half * half_s:n_state + (half + 1) * half_s], ci_ref[half])
            ys.append(y)
        y = jnp.concatenate(ys, axis=-1) + d_ref[...] * u_cols(0, n_lane_chunks, slice(r0, r0 + rc))
        z = jax.nn.gelu(y)
        for j in range(n_lane_chunks):
            z_tm[j, r0:r0 + rc, :] = z[:, j * LANES:(j + 1) * LANES]

    def to_batch_major(b, c):
        z_ref[b] = jnp.concatenate(
            [z_tm[j, pl.ds(b, tc, stride=nb), :] for j in range(n_lane_chunks)], axis=-1)
        return c

    lax.fori_loop(0, nb, to_batch_major, 0)


def _ssm(u, h0_re, h0_im, lb_re, lb_im, b_re, b_im, c_re, c_im, d_vec, *, tc):
    nb, t, ssm_w = u.shape
    n_state = lb_re.shape[1]
    assert t % tc == 0 and nb % 8 == 0
    state = pl.BlockSpec((nb, n_state), lambda i: (0, 0))
    seq = pl.BlockSpec((nb, tc, ssm_w), lambda i: (0, i, 0))
    whole = lambda a: pl.BlockSpec(a.shape, lambda i: (0,) * a.ndim)
    return pl.pallas_call(
        functools.partial(_ssm_kernel, nb=nb, tc=tc),
        grid=(t // tc,),
        in_specs=[seq, state, state, whole(lb_re), whole(lb_im), whole(b_re), whole(b_im),
                  whole(c_re), whole(c_im), whole(d_vec)],
        out_specs=[seq, state, state],
        out_shape=[jax.ShapeDtypeStruct(u.shape, F32), jax.ShapeDtypeStruct((nb, n_state), F32),
                   jax.ShapeDtypeStruct((nb, n_state), F32)],
        scratch_shapes=[pltpu.VMEM((ssm_w // LANES, nb * tc, LANES), F32),
                        pltpu.VMEM((nb * tc, 2 * n_state), F32),
                        pltpu.VMEM((ssm_w // LANES, nb * tc, LANES), F32)],
        compiler_params=pltpu.CompilerParams(dimension_semantics=("arbitrary",),
                                             vmem_limit_bytes=V7X_VMEM_LIMIT),
        name="ssm",
    )(u, h0_re, h0_im, lb_re, lb_im, b_re, b_im, c_re, c_im, d_vec)


def _merge_kernel(x_ref, on_ref, z_ref, wgate_ref, wau_ref, wglu_ref, wout_ref, g_ref, b_ref,
                  o_ref, *, alpha):
    d = x_ref.shape[1]
    x = x_ref[...]
    gates = jax.nn.sigmoid(_dot(x.astype(BF16), wgate_ref[...]))
    a_branch = _dot(on_ref[...].astype(BF16), wau_ref[...])
    glu = _dot(z_ref[...].astype(BF16), wglu_ref[...])
    b_branch = glu[:, :d] * jax.nn.sigmoid(glu[:, d:])
    mix_in = (gates[:, :d] * a_branch + gates[:, d:] * b_branch).astype(BF16)
    mix = _dot(mix_in, wout_ref[...])
    o_ref[...] = _layer_norm(alpha * x + mix, g_ref[...], b_ref[...])


def _merge(x, o_n, z, w_gate, w_att_up, w_glu, w_out, ln_g, ln_b, *, tm, alpha):
    n, d = x.shape
    row = lambda a: pl.BlockSpec((tm, a.shape[1]), lambda i: (i, 0))
    return pl.pallas_call(
        functools.partial(_merge_kernel, alpha=alpha),
        grid=(n // tm,),
        in_specs=[row(x), row(o_n), row(z), _resident(w_gate.shape), _resident(w_att_up.shape),
                  _resident(w_glu.shape), _resident(w_out.shape), _resident((1, d)), _resident((1, d))],
        out_specs=row(x),
        out_shape=jax.ShapeDtypeStruct(x.shape, F32),
        compiler_params=pltpu.CompilerParams(dimension_semantics=("parallel",),
                                             vmem_limit_bytes=V7X_VMEM_LIMIT),
        name="merge",
    )(x, o_n, z, w_gate, w_att_up, w_glu, w_out, ln_g, ln_b)


def _ffn_kernel(x_ref, wup_ref, wdn_ref, g_ref, b_ref, o_ref, *, alpha, ff_chunk):
    x = x_ref[...]
    xb = x.astype(BF16)
    acc = alpha * x
    for c0 in range(0, wup_ref.shape[1], ff_chunk):
        hid = jnp.maximum(_dot(xb, wup_ref[:, c0:c0 + ff_chunk]), 0.0)
        acc = acc + _dot((hid * hid).astype(BF16), wdn_ref[c0:c0 + ff_chunk, :])
    o_ref[...] = _layer_norm(acc, g_ref[...], b_ref[...])


def _ffn(x, w_up, w_down, ln_g, ln_b, *, tm, alpha):
    n, d = x.shape
    row = pl.BlockSpec((tm, d), lambda i: (i, 0))
    return pl.pallas_call(
        functools.partial(_ffn_kernel, alpha=alpha, ff_chunk=d),
        grid=(n // tm,),
        in_specs=[row, _resident(w_up.shape), _resident(w_down.shape), _resident((1, d)),
                  _resident((1, d))],
        out_specs=row,
        out_shape=jax.ShapeDtypeStruct(x.shape, F32),
        compiler_params=pltpu.CompilerParams(dimension_semantics=("parallel",),
                                             vmem_limit_bytes=V7X_VMEM_LIMIT),
        name="ffn",
    )(x, w_up, w_down, ln_g, ln_b)


def _ffn_skip_meta(x, w_up, w_down, ln_g, ln_b, *, tm, alpha):
    bsz, t, d = x.shape
    seq = t - N_META
    assert seq % tm == 0
    window = pl.BlockSpec((pl.Squeezed(), pl.Element(tm), pl.Element(d)),
                          lambda b, j: (b, pl.multiple_of(N_META + j * tm, N_META), 0))
    return pl.pallas_call(
        functools.partial(_ffn_kernel, alpha=alpha, ff_chunk=d),
        grid=(bsz, seq // tm),
        in_specs=[window, _resident(w_up.shape), _resident(w_down.shape), _resident((1, d)),
                  _resident((1, d))],
        out_specs=pl.BlockSpec((None, tm, d), lambda b, j: (b, j, 0)),
        out_shape=jax.ShapeDtypeStruct((bsz, seq, d), F32),
        compiler_params=pltpu.CompilerParams(dimension_semantics=("parallel", "parallel"),
                                             vmem_limit_bytes=V7X_VMEM_LIMIT),
        name="ffn_skip_meta",
    )(x, w_up, w_down, ln_g, ln_b)


def _ssm_tables(a_re, a_im, log_dt, b_re, b_im, c_re, c_im, d):
    n_groups, n_p = a_re.shape
    a_re = a_re.astype(F32)
    a_im = a_im.astype(F32)
    dt = jnp.exp(log_dt.astype(F32))[:, None]
    mag = jnp.exp(a_re * dt)
    lb_re = mag * jnp.cos(a_im * dt)
    lb_im = mag * jnp.sin(a_im * dt)
    nr, ni = lb_re - 1.0, lb_im
    den = a_re * a_re + a_im * a_im
    f_re = (nr * a_re + ni * a_im) / den
    f_im = (ni * a_re - nr * a_im) / den
    b_re = b_re.astype(F32)
    b_im = b_im.astype(F32)
    bb_re = f_re[..., None] * b_re - f_im[..., None] * b_im
    bb_im = f_re[..., None] * b_im + f_im[..., None] * b_re
    eye = jnp.eye(n_groups, dtype=F32)
    hg = n_groups // 2

    def in_mat(bb):
        full = jnp.einsum('gpc,gh->gchp', bb, eye).reshape(n_groups * GROUP_CH, n_groups * n_p)
        hu, hs = hg * GROUP_CH, hg * n_p
        return jnp.stack([full[:hu, :hs], full[hu:, hs:]]).astype(BF16)

    def out_mat(cc):
        full = jnp.einsum('gcp,gh->gphc', cc, eye).reshape(n_groups * n_p, n_groups * GROUP_CH)
        hu, hs = hg * GROUP_CH, hg * n_p
        return jnp.stack([full[:hs, :hu], full[hs:, hu:]]).astype(BF16)

    lb = (jnp.broadcast_to(lb_re.reshape(1, -1), (8, n_groups * n_p)),
          jnp.broadcast_to(lb_im.reshape(1, -1), (8, n_groups * n_p)))
    return (lb, in_mat(bb_re), in_mat(bb_im), out_mat(c_re.astype(F32)), out_mat(-c_im.astype(F32)),
            d.astype(F32).reshape(1, -1))


def kernel(x_prompt, x_sample, cache_k, cache_v, state_ssm_re, state_ssm_im, page_table, meta_tokens, ln_in_g, ln_in_b, rel_bias, w_in, lambda_q1, lambda_k1, lambda_q2, lambda_k2, subln_g, w_att_up, ssm_a_re, ssm_a_im, ssm_log_dt, ssm_b_re, ssm_b_im, ssm_c_re, ssm_c_im, ssm_d, w_glu, w_out, ln1_g, ln1_b, w_up, w_down, ln2_g, ln2_b):
    bsz, seq, d_model = x_prompt.shape
    dec_b, dec_seq, _ = x_sample.shape
    depth = w_in.shape[0]
    att_w = w_att_up.shape[1]
    ssm_w = w_glu.shape[1]
    n_heads = att_w // HEAD_WIDTH
    n_groups = ssm_w // GROUP_CH
    n_state = n_groups * STATE_DIM
    t = N_META + seq
    past = page_table.shape[1] * PAGE_SIZE
    alpha = (2.0 * depth) ** 0.25
    assert seq % ATT_Q_TILE == 0 and t % SSM_CHUNK == 0 and past >= 2 * PAGE_SIZE

    tm_p = _row_tile(t, 704)
    tm_s = _row_tile(dec_b * dec_seq, 512)

    xp = x_prompt
    kt_all = v_all = None
    xs = x_sample.reshape(dec_b * dec_seq, d_model)
    row = lambda a: a.astype(F32).reshape(1, -1)
    near, mb, mm, bias_s = _bias_tables(rel_bias, ATT_Q_TILE, dec_seq)
    n_pool = cache_k.shape[1]
    cache_kt = jnp.transpose(cache_k, (0, 1, 3, 4, 5, 2)).reshape(depth, n_pool, att_w, PAGE_SIZE)
    cache_v = cache_v.reshape(depth, n_pool, PAGE_SIZE * n_heads, HEAD_WIDTH)
    h0p = jnp.zeros((bsz, n_state), F32)

    outs = {name: [] for name in ("hpr", "hpi", "ks", "vs", "hsr", "hsi")}
    for l in range(depth):
        lam_init = 0.8 - 0.6 * math.exp(-0.3 * l)
        lam = (jnp.exp(jnp.sum(lambda_q1[l].astype(F32) * lambda_k1[l].astype(F32)))
               - jnp.exp(jnp.sum(lambda_q2[l].astype(F32) * lambda_k2[l].astype(F32))) + lam_init)
        lam = lam.reshape(1)
        sub_scale = 1.0 - lam_init
        g_sub = row(subln_g[l])
        n_qkvu = 3 * att_w + ssm_w
        w_qkvu_bf = w_in[l, :, :n_qkvu].astype(BF16)
        w_gate_bf = w_in[l, :, n_qkvu:].astype(BF16)
        w_au_bf = w_att_up[l].astype(BF16)
        w_glu_bf = w_glu[l].astype(BF16)
        w_out_bf = w_out[l].astype(BF16)
        w_up_bf = w_up[l].astype(BF16)
        w_dn_bf = w_down[l].astype(BF16)
        lb, sb_re, sb_im, sc_re, sc_im, d_vec = _ssm_tables(
            ssm_a_re[l], ssm_a_im[l], ssm_log_dt[l], ssm_b_re[l], ssm_b_im[l], ssm_c_re[l],
            ssm_c_im[l], ssm_d[l])
        ln = dict(apply_ln=(l == 0), att_w=att_w, ssm_w=ssm_w)

        *x_ln, q, kf, vb, u, v_all = _in_proj_prompt(
            xp, meta_tokens.astype(F32), row(ln_in_g), row(ln_in_b), w_qkvu_bf, v_all, layer=l,
            depth=depth, bsz=bsz, t=t, tm=tm_p, att_w=att_w, ssm_w=ssm_w)
        xp = x_ln[0] if x_ln else xp
        seq3 = lambda a: a.reshape(bsz, t, a.shape[-1])
        o_n, kt_all = _attn_prompt(lam, seq3(q), seq3(kf), seq3(vb), near, mb, mm, g_sub, kt_all,
                                   layer=l, depth=depth, sub_scale=sub_scale)
        z, hpr, hpi = _ssm(seq3(u), h0p, h0p, *lb, sb_re, sb_im, sc_re, sc_im, d_vec,
                           tc=SSM_CHUNK)
        xp = _merge(xp, o_n.reshape(bsz * t, att_w), z.reshape(bsz * t, ssm_w), w_gate_bf, w_au_bf,
                    w_glu_bf, w_out_bf, row(ln1_g[l]), row(ln1_b[l]), tm=tm_p, alpha=alpha)
        if l < depth - 1:
            xp = _ffn(xp, w_up_bf, w_dn_bf, row(ln2_g[l]), row(ln2_b[l]), tm=tm_p, alpha=alpha)
        else:
            y_prompt = _ffn_skip_meta(xp.reshape(bsz, t, d_model), w_up_bf, w_dn_bf, row(ln2_g[l]),
                                      row(ln2_b[l]), tm=_row_tile(seq, 512), alpha=alpha)
        outs["hpr"].append(hpr.reshape(bsz, n_groups, STATE_DIM))
        outs["hpi"].append(hpi.reshape(bsz, n_groups, STATE_DIM))

        *x_ln, q, kf, vf, u = _in_proj(xs, row(ln_in_g), row(ln_in_b), w_qkvu_bf, tm=tm_s, **ln)
        xs = x_ln[0] if x_ln else xs
        dec3 = lambda a: a.reshape(dec_b, dec_seq, a.shape[-1])
        o_n = _attn_sample(page_table, lam, dec3(q.astype(F32)), dec3(kf), dec3(vf), bias_s, g_sub,
                           cache_kt, cache_v, layer=l, sub_scale=sub_scale)
        z, hsr, hsi = _ssm(dec3(u), state_ssm_re[l].reshape(dec_b, n_state).astype(F32),
                           state_ssm_im[l].reshape(dec_b, n_state).astype(F32), *lb, sb_re,
                           sb_im, sc_re, sc_im, d_vec, tc=dec_seq)
        xs = _merge(xs, o_n.reshape(dec_b * dec_seq, att_w), z.reshape(dec_b * dec_seq, ssm_w),
                    w_gate_bf, w_au_bf, w_glu_bf, w_out_bf, row(ln1_g[l]), row(ln1_b[l]), tm=tm_s,
                    alpha=alpha)
        xs = _ffn(xs, w_up_bf, w_dn_bf, row(ln2_g[l]), row(ln2_b[l]), tm=tm_s, alpha=alpha)
        outs["ks"].append(kf.reshape(dec_b, dec_seq, n_heads, 2, HEAD_DIM))
        outs["vs"].append(vf.reshape(dec_b, dec_seq, n_heads, HEAD_WIDTH))
        outs["hsr"].append(hsr.reshape(dec_b, n_groups, STATE_DIM))
        outs["hsi"].append(hsi.reshape(dec_b, n_groups, STATE_DIM))

    y_sample = xs.reshape(dec_b, dec_seq, d_model)
    st = lambda name: jnp.stack(outs[name])
    k_prompt = jnp.transpose(kt_all.reshape(depth, bsz, n_heads, 2, HEAD_DIM, t), (0, 1, 5, 2, 3, 4))
    v_prompt = v_all.reshape(depth, bsz, t, n_heads, HEAD_WIDTH)
    return (y_prompt, y_sample, k_prompt, v_prompt, st("hpr"), st("hpi"), st("ks"), st("vs"),
            st("hsr"), st("hsi"))
```

```python
import functools
import math

import jax
import jax.numpy as jnp
from jax import lax
from jax.experimental import pallas as pl
from jax.experimental.pallas import tpu as pltpu

F32 = jnp.float32
BF16 = jnp.bfloat16

N_META = 16
HEAD_DIM = 64
HEAD_WIDTH = 2 * HEAD_DIM
GROUP_CH = 16
STATE_DIM = 64
PAGE_SIZE = 128
LANES = 128
N_BUCKETS = 32
MAX_DISTANCE = 128
EPS = 1e-5
LOG2E = math.log2(math.e)

ATT_Q_TILE = 256
ATT_LOOKAHEAD = 2
SSM_CHUNK = 48
V7X_VMEM_LIMIT = 56 * 1024 * 1024

assert (N_BUCKETS // 2 + int(math.log((MAX_DISTANCE + 1) / (N_BUCKETS // 2))
                             / math.log(MAX_DISTANCE / (N_BUCKETS // 2)) * (N_BUCKETS // 2))
        >= N_BUCKETS - 1)


def _layer_norm(x, g, b):
    mu = jnp.mean(x, -1, keepdims=True)
    xc = x - mu
    var = jnp.mean(xc * xc, -1, keepdims=True)
    return xc * lax.rsqrt(var + EPS) * g + b


def _dot(a, b):
    return jnp.dot(a, b, preferred_element_type=F32)


def _dot_nt(a, b):
    return lax.dot_general(a, b, (((1,), (1,)), ((), ())), preferred_element_type=F32)


def _resident(shape):
    return pl.BlockSpec(shape, lambda *_: (0,) * len(shape), pipeline_mode=pl.Buffered(1))


def _row_tile(n_rows, target):
    best = None
    for t in range(16, min(n_rows, target) + 1, 16):
        if n_rows % t == 0:
            best = t
    assert best is not None, n_rows
    return best


def _project_qkvu(x, w_ref, att_w, ssm_w):
    xb = x.astype(BF16)
    widths = (att_w, att_w, att_w, ssm_w)
    starts = [sum(widths[:i]) for i in range(len(widths))]
    q, k, v, u = (_dot(xb, w_ref[:, c0:c0 + w]) for c0, w in zip(starts, widths))
    return (q * (HEAD_DIM ** -0.5 * LOG2E)).astype(BF16), k, v, u


def _in_proj_kernel(x_ref, g_ref, b_ref, w_ref, *out_refs, apply_ln, att_w, ssm_w):
    x = x_ref[...]
    if apply_ln:
        xo_ref, *out_refs = out_refs
        x = _layer_norm(x, g_ref[...], b_ref[...])
        xo_ref[...] = x
    q_ref, k_ref, v_ref, u_ref = out_refs
    q_ref[...], k_ref[...], v_ref[...], u_ref[...] = _project_qkvu(x, w_ref, att_w, ssm_w)


def _in_proj(x, ln_g, ln_b, w_qkvu, *, apply_ln, tm, att_w, ssm_w):
    n, d = x.shape
    row = lambda w: pl.BlockSpec((tm, w), lambda i: (i, 0))
    outs = [(att_w, BF16), (att_w, F32), (att_w, F32), (ssm_w, F32)]
    if apply_ln:
        outs = [(d, F32)] + outs
    return pl.pallas_call(
        functools.partial(_in_proj_kernel, apply_ln=apply_ln, att_w=att_w, ssm_w=ssm_w),
        grid=(n // tm,),
        in_specs=[row(d), _resident((1, d)), _resident((1, d)), _resident(w_qkvu.shape)],
        out_specs=[row(w) for w, _ in outs],
        out_shape=[jax.ShapeDtypeStruct((n, w), dt) for w, dt in outs],
        compiler_params=pltpu.CompilerParams(dimension_semantics=("parallel",),
                                             vmem_limit_bytes=V7X_VMEM_LIMIT),
        name="in_proj",
    )(x, ln_g, ln_b, w_qkvu)


def _in_proj_prompt_kernel(*refs, first, tm, att_w, ssm_w):
    if first:
        x_ref, meta_ref, g_ref, b_ref, w_ref, xo_ref, q_ref, k_ref, vb_ref, u_ref, vall_ref = refs
    else:
        x_ref, w_ref, _, q_ref, k_ref, vb_ref, u_ref, vall_ref = refs
    n_heads = att_w // HEAD_WIDTH

    def body(x):
        if first:
            x = _layer_norm(x, g_ref[...], b_ref[...])
            xo_ref[...] = x
        q_ref[...], k_ref[...], v, u_ref[...] = _project_qkvu(x, w_ref, att_w, ssm_w)
        vb_ref[...] = v.astype(BF16)
        for h in range(n_heads):
            vall_ref[pl.ds(h, tm, stride=n_heads), :] = v[:, h * HEAD_WIDTH:(h + 1) * HEAD_WIDTH]

    if first:
        @pl.when(pl.program_id(1) == 0)
        def _():
            body(jnp.concatenate([meta_ref[...], x_ref[:tm - N_META, :]], axis=0))

        @pl.when(pl.program_id(1) != 0)
        def _():
            body(x_ref[...])
    else:
        body(x_ref[...])


def _in_proj_prompt(x, meta, ln_g, ln_b, w_qkvu, v_all, *, layer, depth, bsz, t, tm, att_w, ssm_w):
    first = layer == 0
    d = x.shape[-1]
    n = bsz * t
    n_j = t // tm
    n_heads = att_w // HEAD_WIDTH
    row = lambda w: pl.BlockSpec((tm, w), lambda b, j: (b * n_j + j, 0))
    outs = [(att_w, BF16), (att_w, F32), (att_w, BF16), (ssm_w, F32)]
    if first:
        window = pl.BlockSpec(
            (pl.Squeezed(), pl.Element(tm), pl.Element(d)),
            lambda b, j: (b, pl.multiple_of(jnp.maximum(j * tm - N_META, 0), N_META), 0))
        in_specs = [window, _resident(meta.shape), _resident((1, d)), _resident((1, d)),
                    _resident(w_qkvu.shape)]
        args = (x, meta, ln_g, ln_b, w_qkvu)
        outs = [(d, F32)] + outs
        aliases = {}
    else:
        in_specs = [row(d), _resident(w_qkvu.shape), pl.BlockSpec(memory_space=pl.ANY)]
        args = (x, w_qkvu, v_all)
        aliases = {2: len(outs)}
    vall_shape = (depth, n * n_heads, HEAD_WIDTH)
    return pl.pallas_call(
        functools.partial(_in_proj_prompt_kernel, first=first, tm=tm, att_w=att_w, ssm_w=ssm_w),
        grid=(bsz, n_j),
        in_specs=in_specs,
        out_specs=[row(w) for w, _ in outs]
        + [pl.BlockSpec((None, tm * n_heads, HEAD_WIDTH), lambda b, j: (layer, b * n_j + j, 0))],
        out_shape=[jax.ShapeDtypeStruct((n, w), dt) for w, dt in outs]
        + [jax.ShapeDtypeStruct(vall_shape, F32)],
        input_output_aliases=aliases,
        compiler_params=pltpu.CompilerParams(dimension_semantics=("parallel", "parallel"),
                                             vmem_limit_bytes=V7X_VMEM_LIMIT),
        name="in_proj_prompt",
    )(*args)


def _bias_kernel(rb_ref, rel_ref, o_ref):
    rel = rel_ref[...]
    n = jnp.maximum(rel, 0)
    exact = N_BUCKETS // 2
    large = exact + (jnp.log(jnp.maximum(n, 1).astype(F32) / exact)
                     / math.log(MAX_DISTANCE / exact) * (N_BUCKETS - exact)).astype(jnp.int32)
    bucket = jnp.where(n < exact, n, jnp.minimum(large, N_BUCKETS - 1))
    for h in range(o_ref.shape[0]):
        far = rb_ref[N_BUCKETS - 1, h]
        acc = jnp.zeros(rel.shape, F32)
        for b in range(N_BUCKETS - 1):
            acc = jnp.where(bucket == b, (rb_ref[b, h] - far) * LOG2E, acc)
        o_ref[h] = jnp.where(rel >= 0, acc, -jnp.inf)


def _bias_tile(rel_bias, rel):
    n_heads = rel_bias.shape[1]
    return pl.pallas_call(
        _bias_kernel,
        in_specs=[pl.BlockSpec(memory_space=pltpu.SMEM), pl.BlockSpec(rel.shape, lambda: (0, 0))],
        out_specs=pl.BlockSpec((n_heads,) + rel.shape, lambda: (0, 0, 0)),
        out_shape=jax.ShapeDtypeStruct((n_heads,) + rel.shape, F32),
        name="bias_tile",
    )(rel_bias.astype(F32), rel)


def _bias_tables(rel_bias, tq, dec_seq):
    i32 = jnp.int32
    n_heads = rel_bias.shape[1]
    r = jnp.arange(tq, dtype=i32)[:, None]
    near = _bias_tile(rel_bias, tq + r - jnp.arange(2 * tq, dtype=i32)[None, :])
    mb = _bias_tile(rel_bias, N_META + r - jnp.arange(N_META, dtype=i32)[None, :])
    mr = jnp.arange(N_META, dtype=i32)
    mm = _bias_tile(rel_bias, mr[:, None] - mr[None, :])
    s = jnp.arange(dec_seq, dtype=i32)[:, None]
    kk = jnp.arange(PAGE_SIZE, dtype=i32)[None, :]
    last = _bias_tile(rel_bias, PAGE_SIZE + s - kk)
    new = _bias_tile(rel_bias, jnp.where(kk < dec_seq, s - kk, -1))

    def rows(tile):
        return jnp.broadcast_to(tile[:, None], (n_heads, 2, dec_seq, PAGE_SIZE)).reshape(-1, PAGE_SIZE)

    return near, mb, mm, jnp.stack([rows(last), rows(new)])


def _softmax_pv(blocks):
    m = functools.reduce(jnp.maximum, [jnp.max(s, -1, keepdims=True) for s, _ in blocks])
    acc = 0.0
    for s, v_ones in blocks:
        acc = acc + _dot(jnp.exp2(s - m).astype(BF16), v_ones)
    return acc[:, :HEAD_WIDTH] / acc[:, HEAD_WIDTH:HEAD_WIDTH + 1]


def _sub_norm(o1, o2, lam, g, sub_scale):
    o = o1 - lam * o2
    return o * lax.rsqrt(jnp.mean(o * o, -1, keepdims=True) + EPS) * g * sub_scale


def _attn_prompt_kernel(lam_ref, q_ref, k_ref, v_ref, near_ref, mb_ref, mm_ref, g_ref, *rest,
                        tq, n_q, sub_scale):
    o_ref, kt_ref, k1_ref, k2_ref, v1_ref = rest[-5:]
    lam = lam_ref[0]
    g = g_ref[...]
    k = k_ref[0]
    t = k.shape[0]
    lane = lax.broadcasted_iota(jnp.int32, k.shape, 1)
    k1_ref[...] = jnp.where(lane < HEAD_DIM, k, 0.0).astype(BF16)
    k2_ref[...] = jnp.where(lane >= HEAD_DIM, k, 0.0).astype(BF16)
    v1_ref[:, :HEAD_WIDTH] = v_ref[0]
    v1_ref[:, HEAD_WIDTH:] = (lane == 0).astype(F32).astype(BF16)

    n_full = t // LANES
    for c in range(n_full):
        kt_ref[:, c * LANES:(c + 1) * LANES] = k_ref[0, c * LANES:(c + 1) * LANES, :].T
    tail = t - n_full * LANES
    if tail:
        k_tail = jnp.concatenate([k_ref[0, n_full * LANES:, :],
                                  jnp.zeros((LANES - tail, HEAD_WIDTH), F32)], axis=0)
        kt_ref[:, n_full * LANES:] = k_tail.T[:, :tail]

    def scores(rows, spans):
        q_t = q_ref[0, rows, :]
        comps = []
        for kc_ref in (k1_ref, k2_ref):
            blocks = []
            for k0, size, bias in spans:
                s = _dot_nt(q_t, kc_ref[k0:k0 + size, :])
                blocks.append((s if bias is None else s + bias, v1_ref[k0:k0 + size, :]))
            comps.append(blocks)
        return comps

    def finish(rows, comps):
        outs = [_softmax_pv(blocks) for blocks in comps]
        o_ref[0, rows, :] = _sub_norm(outs[0], outs[1], lam, g, sub_scale).astype(BF16)

    tiles = [(slice(0, N_META), [(0, N_META, mm_ref[0])])]
    for i in range(n_q):
        r0 = N_META + i * tq
        if i == 0:
            spans = [(0, N_META, mb_ref[0]), (r0, tq, near_ref[0, :, tq:])]
        else:
            spans = [(0, r0 - tq, None), (r0 - tq, 2 * tq, near_ref[0])]
        tiles.append((slice(r0, r0 + tq), spans))
    pending = []
    for rows, spans in tiles:
        pending.append((rows, scores(rows, spans)))
        if len(pending) > ATT_LOOKAHEAD:
            finish(*pending.pop(0))
    for item in pending:
        finish(*item)


def _attn_prompt(lam, q, k, vb, near, mb, mm, g, kt_all, *, layer, depth, sub_scale):
    bsz, t, att_w = q.shape
    n_heads = att_w // HEAD_WIDTH
    tq = near.shape[1]
    n_q = (t - N_META) // tq
    assert n_q * tq + N_META == t and tq > MAX_DISTANCE
    smem = pl.BlockSpec(memory_space=pltpu.SMEM)
    seq = pl.BlockSpec((1, t, HEAD_WIDTH), lambda b, h: (b, 0, h))
    per_head = lambda a: pl.BlockSpec((1,) + a.shape[1:], lambda b, h: (h, 0, 0))
    in_specs = [smem, seq, seq, seq, per_head(near), per_head(mb), per_head(mm),
                pl.BlockSpec((1, HEAD_WIDTH), lambda b, h: (0, 0))]
    args = (lam, q, k, vb, near, mb, mm, g)
    aliases = {}
    if kt_all is not None:
        in_specs.append(pl.BlockSpec(memory_space=pl.ANY))
        args += (kt_all,)
        aliases = {len(args) - 1: 1}
    return pl.pallas_call(
        functools.partial(_attn_prompt_kernel, tq=tq, n_q=n_q, sub_scale=sub_scale),
        grid=(bsz, n_heads),
        in_specs=in_specs,
        out_specs=[seq, pl.BlockSpec((None, None, HEAD_WIDTH, t), lambda b, h: (layer, b, h, 0))],
        out_shape=[jax.ShapeDtypeStruct(q.shape, BF16),
                   jax.ShapeDtypeStruct((depth, bsz, att_w, t), F32)],
        input_output_aliases=aliases,
        scratch_shapes=[pltpu.VMEM((t, HEAD_WIDTH), BF16), pltpu.VMEM((t, HEAD_WIDTH), BF16),
                        pltpu.VMEM((t, 2 * HEAD_WIDTH), BF16)],
        compiler_params=pltpu.CompilerParams(dimension_semantics=("parallel", "parallel"),
                                             vmem_limit_bytes=V7X_VMEM_LIMIT),
        name="attn_prompt",
    )(*args)


def _attn_sample_kernel(pt_ref, lam_ref, q_ref, kn_ref, vn_ref, bias_ref, g_ref, *rest,
                        n_pages, n_heads, dec_seq, sub_scale):
    kt_pages = rest[:n_pages]
    v_pages = rest[n_pages:2 * n_pages]
    o_ref, s_ref = rest[2 * n_pages:]
    del pt_ref
    lam = lam_ref[0]
    att_w = q_ref.shape[2]
    n_rows = 2 * n_heads * dec_seq
    head_rows = 2 * dec_seq
    q = q_ref[0]
    q_rep = jnp.concatenate([q] * (2 * n_heads), axis=0)
    row_blk = lax.broadcasted_iota(jnp.int32, (n_rows, att_w), 0) // dec_seq
    col_blk = lax.broadcasted_iota(jnp.int32, (n_rows, att_w), 1) // HEAD_DIM
    q_exp = jnp.where(row_blk == col_blk, q_rep, 0.0).astype(BF16)

    pad = jnp.zeros((PAGE_SIZE - dec_seq, att_w), F32)
    k_new = jnp.concatenate([kn_ref[0], pad], axis=0).astype(BF16)
    v_new = jnp.concatenate([vn_ref[0], pad], axis=0).astype(BF16)

    m_el = None
    for p in range(n_pages + 1):
        if p < n_pages:
            s = _dot(q_exp, kt_pages[p][...].astype(BF16))
            if p == n_pages - 1:
                s = s + bias_ref[0]
        else:
            s = _dot_nt(q_exp, k_new) + bias_ref[1]
        s_ref[p] = s
        m_el = s if m_el is None else jnp.maximum(m_el, s)
    m = jnp.max(m_el, -1, keepdims=True)

    l_el = jnp.zeros((n_rows, PAGE_SIZE), F32)
    accs = [jnp.zeros((head_rows, HEAD_WIDTH), F32) for _ in range(n_heads)]
    for p in range(n_pages + 1):
        e = jnp.exp2(s_ref[p] - m)
        l_el = l_el + e
        eb = e.astype(BF16)
        for h in range(n_heads):
            if p < n_pages:
                v_h = v_pages[p][pl.ds(h, PAGE_SIZE, stride=n_heads), :].astype(BF16)
            else:
                v_h = v_new[:, h * HEAD_WIDTH:(h + 1) * HEAD_WIDTH]
            accs[h] = accs[h] + _dot(eb[h * head_rows:(h + 1) * head_rows, :], v_h)
    l = jnp.sum(l_el, -1, keepdims=True)

    g = g_ref[...]
    heads = []
    for h in range(n_heads):
        o = accs[h] / l[h * head_rows:(h + 1) * head_rows]
        heads.append(_sub_norm(o[:dec_seq], o[dec_seq:], lam, g, sub_scale))
    o_ref[0] = jnp.concatenate(heads, axis=-1)


def _attn_sample(page_table, lam, q, k_new, v_new, bias, g, cache_kt, cache_v, *, layer, sub_scale):
    dec_b, dec_seq, att_w = q.shape
    n_pages = page_table.shape[1]
    n_heads = att_w // HEAD_WIDTH
    n_rows = 2 * n_heads * dec_seq
    pt_flat = page_table.reshape(-1)
    per_b = pl.BlockSpec((1, dec_seq, att_w), lambda b, pt: (b, 0, 0))

    def page_spec(p):
        return pl.BlockSpec((None, None) + cache_v.shape[2:],
                            lambda b, pt: (layer, pt[b * n_pages + p], 0, 0))

    pages = [page_spec(p) for p in range(n_pages)]
    grid_spec = pltpu.PrefetchScalarGridSpec(
        num_scalar_prefetch=1,
        grid=(dec_b,),
        in_specs=[pl.BlockSpec(memory_space=pltpu.SMEM), per_b, per_b, per_b,
                  pl.BlockSpec(bias.shape, lambda b, pt: (0, 0, 0)),
                  pl.BlockSpec((1, HEAD_WIDTH), lambda b, pt: (0, 0))] + pages + pages,
        out_specs=per_b,
        scratch_shapes=[pltpu.VMEM((n_pages + 1, n_rows, PAGE_SIZE), F32)],
    )
    return pl.pallas_call(
        functools.partial(_attn_sample_kernel, n_pages=n_pages, n_heads=n_heads, dec_seq=dec_seq,
                          sub_scale=sub_scale),
        grid_spec=grid_spec,
        out_shape=jax.ShapeDtypeStruct(q.shape, F32),
        compiler_params=pltpu.CompilerParams(dimension_semantics=("parallel",),
                                             vmem_limit_bytes=V7X_VMEM_LIMIT),
        name="attn_sample",
    )(pt_flat, lam, q, k_new, v_new, bias, g, *([cache_kt] * n_pages), *([cache_v] * n_pages))


def _ssm_kernel(u_ref, h0r_ref, h0i_ref, lbr_ref, lbi_ref, br_ref, bi_ref, cr_ref, ci_ref, d_ref,
                z_ref, hr_ref, hi_ref, u_tm, z_tm, *x_scratch, nb, tc):
    n_parts = u_tm.shape[0]
    part_s = lbr_ref.shape[1] // n_parts
    n_rows = nb * tc
    x_re, x_im = x_scratch[:n_parts], x_scratch[n_parts:]

    @pl.when(pl.program_id(0) == 0)
    def _():
        hr_ref[...] = h0r_ref[...]
        hi_ref[...] = h0i_ref[...]

    def to_time_major(b, c):
        ub = u_ref[b]
        for p in range(n_parts):
            u_tm[p, pl.ds(b, tc, stride=nb), :] = ub[:, p * LANES:(p + 1) * LANES]
        return c

    lax.fori_loop(0, nb, to_time_major, 0, unroll=(nb == 8))

    def project_in(p):
        up = u_tm[p].astype(BF16)
        x_re[p][...] = _dot(up, br_ref[p])
        x_im[p][...] = _dot(up, bi_ref[p])

    def recur(p):
        gcols = slice(p * part_s, (p + 1) * part_s)
        lr = lbr_ref[:, gcols]
        li = lbi_ref[:, gcols]

        def rows(r0):
            hr = hr_ref[pl.ds(r0, 8), gcols]
            hi = hi_ref[pl.ds(r0, 8), gcols]
            for t in range(tc):
                row = pl.ds(t * nb + r0, 8)
                hr, hi = (lr * hr - li * hi + x_re[p][row, :], lr * hi + li * hr + x_im[p][row, :])
                x_re[p][row, :] = hr
                x_im[p][row, :] = hi
            hr_ref[pl.ds(r0, 8), gcols] = hr
            hi_ref[pl.ds(r0, 8), gcols] = hi

        if nb == 8:
            rows(0)
        else:
            lax.fori_loop(0, nb // 8, lambda rg, c: (rows(pl.multiple_of(rg * 8, 8)), c)[1], 0)

    def project_out(p):
        rc = n_rows if n_rows <= 512 else 256
        for r0 in range(0, n_rows, rc):
            rows = slice(r0, r0 + rc)
            y = (_dot(x_re[p][rows, :].astype(BF16), cr_ref[p])
                 + _dot(x_im[p][rows, :].astype(BF16), ci_ref[p])
                 + d_ref[:, p * LANES:(p + 1) * LANES] * u_tm[p, rows, :])
            z_tm[p, rows, :] = jax.nn.gelu(y)

    for step in range(n_parts + 2):
        if step < n_parts:
            project_in(step)
        if 1 <= step <= n_parts:
            recur(step - 1)
        if step >= 2:
            project_out(step - 2)

    def to_batch_major(b, c):
        z_ref[b] = jnp.concatenate(
            [z_tm[p, pl.ds(b, tc, stride=nb), :] for p in range(n_parts)], axis=-1)
        return c

    lax.fori_loop(0, nb, to_batch_major, 0, unroll=(nb == 8))


def _ssm(u, h0_re, h0_im, lb_re, lb_im, b_re, b_im, c_re, c_im, d_vec, *, tc):
    nb, t, ssm_w = u.shape
    n_state = lb_re.shape[1]
    n_parts = ssm_w // LANES
    assert t % tc == 0 and nb % 8 == 0 and b_re.shape == (n_parts, LANES, n_state // n_parts)
    state =pl.BlockSpec((nb, n_state), lambda i: (0, 0))
    seq = pl.BlockSpec((nb, tc, ssm_w), lambda i: (0, i, 0))
    whole = lambda a: pl.BlockSpec(a.shape, lambda i: (0,) * a.ndim)
    return pl.pallas_call(
        functools.partial(_ssm_kernel, nb=nb, tc=tc),
        grid=(t // tc,),
        in_specs=[seq, state, state, whole(lb_re), whole(lb_im), whole(b_re), whole(b_im),
                  whole(c_re), whole(c_im), whole(d_vec)],
        out_specs=[seq, state, state],
        out_shape=[jax.ShapeDtypeStruct(u.shape, F32), jax.ShapeDtypeStruct((nb, n_state), F32),
                   jax.ShapeDtypeStruct((nb, n_state), F32)],
        scratch_shapes=[pltpu.VMEM((n_parts, nb * tc, LANES), F32)] * 2
        + [pltpu.VMEM((nb * tc, n_state // n_parts), F32)] * (2 * n_parts),
        compiler_params=pltpu.CompilerParams(dimension_semantics=("arbitrary",),
                                             vmem_limit_bytes=V7X_VMEM_LIMIT),
        name="ssm",
    )(u, h0_re, h0_im, lb_re, lb_im, b_re, b_im, c_re, c_im, d_vec)


def _merge_kernel(x_ref, on_ref, z_ref, wgate_ref, wau_ref, wglu_ref, wout_ref, g_ref, b_ref,
                  o_ref, *, alpha):
    d = x_ref.shape[1]
    x = x_ref[...]
    gates = jax.nn.sigmoid(_dot(x.astype(BF16), wgate_ref[...]))
    a_branch = _dot(on_ref[...].astype(BF16), wau_ref[...])
    glu = _dot(z_ref[...].astype(BF16), wglu_ref[...])
    b_branch = glu[:, :d] * jax.nn.sigmoid(glu[:, d:])
    mix_in = (gates[:, :d] * a_branch + gates[:, d:] * b_branch).astype(BF16)
    mix = _dot(mix_in, wout_ref[...])
    o_ref[...] = _layer_norm(alpha * x + mix, g_ref[...], b_ref[...])


def _merge(x, o_n, z, w_gate, w_att_up, w_glu, w_out, ln_g, ln_b, *, tm, alpha):
    n, d = x.shape
    row = lambda a: pl.BlockSpec((tm, a.shape[1]), lambda i: (i, 0))
    return pl.pallas_call(
        functools.partial(_merge_kernel, alpha=alpha),
        grid=(n // tm,),
        in_specs=[row(x), row(o_n), row(z), _resident(w_gate.shape), _resident(w_att_up.shape),
                  _resident(w_glu.shape), _resident(w_out.shape), _resident((1, d)), _resident((1, d))],
        out_specs=row(x),
        out_shape=jax.ShapeDtypeStruct(x.shape, F32),
        compiler_params=pltpu.CompilerParams(dimension_semantics=("parallel",),
                                             vmem_limit_bytes=V7X_VMEM_LIMIT),
        name="merge",
    )(x, o_n, z, w_gate, w_att_up, w_glu, w_out, ln_g, ln_b)


def _ffn_kernel(x_ref, wup_ref, wdn_ref, g_ref, b_ref, o_ref, *, alpha, ff_chunk):
    x = x_ref[...]
    xb = x.astype(BF16)
    acc = alpha * x
    for c0 in range(0, wup_ref.shape[1], ff_chunk):
        hid = jnp.maximum(_dot(xb, wup_ref[:, c0:c0 + ff_chunk]), 0.0)
        acc = acc + _dot((hid * hid).astype(BF16), wdn_ref[c0:c0 + ff_chunk, :])
    o_ref[...] = _layer_norm(acc, g_ref[...], b_ref[...])


def _ffn(x, w_up, w_down, ln_g, ln_b, *, tm, alpha):
    n, d = x.shape
    row = pl.BlockSpec((tm, d), lambda i: (i, 0))
    return pl.pallas_call(
        functools.partial(_ffn_kernel, alpha=alpha, ff_chunk=d),
        grid=(n // tm,),
        in_specs=[row, _resident(w_up.shape), _resident(w_down.shape), _resident((1, d)),
                  _resident((1, d))],
        out_specs=row,
        out_shape=jax.ShapeDtypeStruct(x.shape, F32),
        compiler_params=pltpu.CompilerParams(dimension_semantics=("parallel",),
                                             vmem_limit_bytes=V7X_VMEM_LIMIT),
        name="ffn",
    )(x, w_up, w_down, ln_g, ln_b)


def _ffn_skip_meta(x, w_up, w_down, ln_g, ln_b, *, tm, alpha):
    bsz, t, d = x.shape
    seq = t - N_META
    assert seq % tm == 0
    window = pl.BlockSpec((pl.Squeezed(), pl.Element(tm), pl.Element(d)),
                          lambda b, j: (b, pl.multiple_of(N_META + j * tm, N_META), 0))
    return pl.pallas_call(
        functools.partial(_ffn_kernel, alpha=alpha, ff_chunk=d),
        grid=(bsz, seq // tm),
        in_specs=[window, _resident(w_up.shape), _resident(w_down.shape), _resident((1, d)),
                  _resident((1, d))],
        out_specs=pl.BlockSpec((None, tm, d), lambda b, j: (b, j, 0)),
        out_shape=jax.ShapeDtypeStruct((bsz, seq, d), F32),
        compiler_params=pltpu.CompilerParams(dimension_semantics=("parallel", "parallel"),
                                             vmem_limit_bytes=V7X_VMEM_LIMIT),
        name="ffn_skip_meta",
    )(x, w_up, w_down, ln_g, ln_b)


def _ssm_tables(a_re, a_im, log_dt, b_re, b_im, c_re, c_im, d):
    n_groups, n_p = a_re.shape
    a_re = a_re.astype(F32)
    a_im = a_im.astype(F32)
    dt = jnp.exp(log_dt.astype(F32))[:, None]
    mag = jnp.exp(a_re * dt)
    lb_re = mag * jnp.cos(a_im * dt)
    lb_im = mag * jnp.sin(a_im * dt)
    nr, ni = lb_re - 1.0, lb_im
    den = a_re * a_re + a_im * a_im
    f_re = (nr * a_re + ni * a_im) / den
    f_im = (ni * a_re - nr * a_im) / den
    b_re = b_re.astype(F32)
    b_im = b_im.astype(F32)
    bb_re = f_re[..., None] * b_re - f_im[..., None] * b_im
    bb_im = f_re[..., None] * b_im + f_im[..., None] * b_re
    gpp = LANES // GROUP_CH
    n_parts = n_groups // gpp
    eye = jnp.eye(gpp, dtype=F32)

    def in_mat(bb):
        bb = bb.reshape(n_parts, gpp, n_p, GROUP_CH)
        return jnp.einsum('ngpc,gh->ngchp', bb, eye).reshape(n_parts, LANES, gpp * n_p).astype(BF16)

    def out_mat(cc):
        cc = cc.reshape(n_parts, gpp, GROUP_CH, n_p)
        return jnp.einsum('ngcp,gh->ngphc', cc, eye).reshape(n_parts, gpp * n_p, LANES).astype(BF16)

    lb = (jnp.broadcast_to(lb_re.reshape(1, -1), (8, n_groups * n_p)),
          jnp.broadcast_to(lb_im.reshape(1, -1), (8, n_groups * n_p)))
    return (lb, in_mat(bb_re), in_mat(bb_im), out_mat(c_re.astype(F32)), out_mat(-c_im.astype(F32)),
            d.astype(F32).reshape(1, -1))


def kernel(x_prompt, x_sample, cache_k, cache_v, state_ssm_re, state_ssm_im, page_table, meta_tokens, ln_in_g, ln_in_b, rel_bias, w_in, lambda_q1, lambda_k1, lambda_q2, lambda_k2, subln_g, w_att_up, ssm_a_re, ssm_a_im, ssm_log_dt, ssm_b_re, ssm_b_im, ssm_c_re, ssm_c_im, ssm_d, w_glu, w_out, ln1_g, ln1_b, w_up, w_down, ln2_g, ln2_b):
    bsz, seq, d_model = x_prompt.shape
    dec_b, dec_seq, _ = x_sample.shape
    depth = w_in.shape[0]
    att_w = w_att_up.shape[1]
    ssm_w = w_glu.shape[1]
    n_heads = att_w // HEAD_WIDTH
    n_groups = ssm_w // GROUP_CH
    n_state = n_groups * STATE_DIM
    t = N_META + seq
    past = page_table.shape[1] * PAGE_SIZE
    alpha = (2.0 * depth) ** 0.25
    assert seq % ATT_Q_TILE == 0 and t % SSM_CHUNK == 0 and past >= 2 * PAGE_SIZE

    tm_p = _row_tile(t, 704)
    tm_s = _row_tile(dec_b * dec_seq, 512)

    xp = x_prompt
    kt_all = v_all = None
    xs = x_sample.reshape(dec_b * dec_seq, d_model)
    row = lambda a: a.astype(F32).reshape(1, -1)
    near, mb, mm, bias_s = _bias_tables(rel_bias, ATT_Q_TILE, dec_seq)
    n_pool = cache_k.shape[1]
    cache_kt = jnp.transpose(cache_k, (0, 1, 3, 4, 5, 2)).reshape(depth, n_pool, att_w, PAGE_SIZE)
    cache_v = cache_v.reshape(depth, n_pool, PAGE_SIZE * n_heads, HEAD_WIDTH)
    h0p = jnp.zeros((bsz, n_state), F32)

    outs = {name: [] for name in ("hpr", "hpi", "ks", "vs", "hsr", "hsi")}
    for l in range(depth):
        lam_init = 0.8 - 0.6 * math.exp(-0.3 * l)
        lam = (jnp.exp(jnp.sum(lambda_q1[l].astype(F32) * lambda_k1[l].astype(F32)))
               - jnp.exp(jnp.sum(lambda_q2[l].astype(F32) * lambda_k2[l].astype(F32))) + lam_init)
        lam = lam.reshape(1)
        sub_scale = 1.0 - lam_init
        g_sub = row(subln_g[l])
        n_qkvu = 3 * att_w + ssm_w
        w_qkvu_bf = w_in[l, :, :n_qkvu].astype(BF16)
        w_gate_bf = w_in[l, :, n_qkvu:].astype(BF16)
        w_au_bf = w_att_up[l].astype(BF16)
        w_glu_bf = w_glu[l].astype(BF16)
        w_out_bf = w_out[l].astype(BF16)
        w_up_bf = w_up[l].astype(BF16)
        w_dn_bf = w_down[l].astype(BF16)
        lb, sb_re, sb_im, sc_re, sc_im, d_vec = _ssm_tables(
            ssm_a_re[l], ssm_a_im[l], ssm_log_dt[l], ssm_b_re[l], ssm_b_im[l], ssm_c_re[l],
            ssm_c_im[l], ssm_d[l])
        ln = dict(apply_ln=(l == 0), att_w=att_w, ssm_w=ssm_w)

        *x_ln, q, kf, vb, u, v_all = _in_proj_prompt(
            xp, meta_tokens.astype(F32), row(ln_in_g), row(ln_in_b), w_qkvu_bf, v_all, layer=l,
            depth=depth, bsz=bsz, t=t, tm=tm_p, att_w=att_w, ssm_w=ssm_w)
        xp = x_ln[0] if x_ln else xp
        seq3 = lambda a: a.reshape(bsz, t, a.shape[-1])
        o_n, kt_all = _attn_prompt(lam, seq3(q), seq3(kf), seq3(vb), near, mb, mm, g_sub, kt_all,
                                   layer=l, depth=depth, sub_scale=sub_scale)
        z, hpr, hpi = _ssm(seq3(u), h0p, h0p, *lb, sb_re, sb_im, sc_re, sc_im, d_vec,
                           tc=SSM_CHUNK)
        xp = _merge(xp, o_n.reshape(bsz * t, att_w), z.reshape(bsz * t, ssm_w), w_gate_bf, w_au_bf,
                    w_glu_bf, w_out_bf, row(ln1_g[l]), row(ln1_b[l]), tm=tm_p, alpha=alpha)
        if l < depth - 1:
            xp = _ffn(xp, w_up_bf, w_dn_bf, row(ln2_g[l]), row(ln2_b[l]), tm=tm_p, alpha=alpha)
        else:
            y_prompt = _ffn_skip_meta(xp.reshape(bsz, t, d_model), w_up_bf, w_dn_bf, row(ln2_g[l]),
                                      row(ln2_b[l]), tm=_row_tile(seq, 512), alpha=alpha)
        outs["hpr"].append(hpr.reshape(bsz, n_groups, STATE_DIM))
        outs["hpi"].append(hpi.reshape(bsz, n_groups, STATE_DIM))

        *x_ln, q, kf, vf, u = _in_proj(xs, row(ln_in_g), row(ln_in_b), w_qkvu_bf, tm=tm_s, **ln)
        xs = x_ln[0] if x_ln else xs
        dec3 = lambda a: a.reshape(dec_b, dec_seq, a.shape[-1])
        o_n = _attn_sample(page_table, lam, dec3(q.astype(F32)), dec3(kf), dec3(vf), bias_s, g_sub,
                           cache_kt, cache_v, layer=l, sub_scale=sub_scale)
        z, hsr, hsi = _ssm(dec3(u), state_ssm_re[l].reshape(dec_b, n_state).astype(F32),
                           state_ssm_im[l].reshape(dec_b, n_state).astype(F32), *lb, sb_re,
                           sb_im, sc_re, sc_im, d_vec, tc=dec_seq)
        xs = _merge(xs, o_n.reshape(dec_b * dec_seq, att_w), z.reshape(dec_b * dec_seq, ssm_w),
                    w_gate_bf, w_au_bf, w_glu_bf, w_out_bf, row(ln1_g[l]), row(ln1_b[l]), tm=tm_s,
                    alpha=alpha)
        xs = _ffn(xs, w_up_bf, w_dn_bf, row(ln2_g[l]), row(ln2_b[l]), tm=tm_s, alpha=alpha)
        outs["ks"].append(kf.reshape(dec_b, dec_seq, n_heads, 2, HEAD_DIM))
        outs["vs"].append(vf.reshape(dec_b, dec_seq, n_heads, HEAD_WIDTH))
        outs["hsr"].append(hsr.reshape(dec_b, n_groups, STATE_DIM))
        outs["hsi"].append(hsi.reshape(dec_b, n_groups, STATE_DIM))

    y_sample = xs.reshape(dec_b, dec_seq, d_model)
    st = lambda name: jnp.stack(outs[name])
    k_prompt = jnp.transpose(kt_all.reshape(depth, bsz, n_heads, 2, HEAD_DIM, t), (0, 1, 5, 2, 3, 4))
    v_prompt = v_all.reshape(depth, bsz, t, n_heads, HEAD_WIDTH)
    return (y_prompt, y_sample, k_prompt, v_prompt, st("hpr"), st("hpi"), st("ks"), st("vs"),
            st("hsr"), st("hsi"))
```

```python
import functools
import math

import jax
import jax.numpy as jnp
from jax import lax
from jax.experimental import pallas as pl
from jax.experimental.pallas import tpu as pltpu

F32 = jnp.float32
BF16 = jnp.bfloat16

N_META = 16
HEAD_DIM = 64
HEAD_WIDTH = 2 * HEAD_DIM
GROUP_CH = 16
STATE_DIM = 64
PAGE_SIZE = 128
LANES = 128
N_BUCKETS = 32
MAX_DISTANCE = 128
EPS = 1e-5
LOG2E = math.log2(math.e)

ATT_Q_TILE = 256
ATT_LOOKAHEAD = 2
MERGE_CHUNK = 256
SSM_CHUNK = 48
V7X_VMEM_BYTES = 64 * 1024 * 1024
V7X_VMEM_LIMIT = V7X_VMEM_BYTES * 7 // 8

assert (N_BUCKETS // 2 + int(math.log((MAX_DISTANCE + 1) / (N_BUCKETS // 2))
                             / math.log(MAX_DISTANCE / (N_BUCKETS // 2)) * (N_BUCKETS // 2))
        >= N_BUCKETS - 1)


def _layer_norm(x, g, b):
    mu = jnp.mean(x, -1, keepdims=True)
    xc = x - mu
    var = jnp.mean(xc * xc, -1, keepdims=True)
    return xc * lax.rsqrt(var + EPS) * g + b


def _dot(a, b):
    return jnp.dot(a, b, preferred_element_type=F32)


def _dot_nt(a, b):
    return lax.dot_general(a, b, (((1,), (1,)), ((), ())), preferred_element_type=F32)


def _resident(shape):
    return pl.BlockSpec(shape, lambda *_: (0,) * len(shape), pipeline_mode=pl.Buffered(1))


def _of_layer(a, layer):
    return pl.BlockSpec((None,) + a.shape[1:], lambda *_: (layer,) + (0,) * (a.ndim - 1),
                        pipeline_mode=pl.Buffered(1))


def _row_tile(n_rows, target):
    best = None
    for t in range(16, min(n_rows, target) + 1, 16):
        if n_rows % t == 0:
            best = t
    assert best is not None, n_rows
    return best


def _project_qkvu(x, w_ref, att_w, ssm_w):
    xb = x.astype(BF16)
    widths = (att_w, att_w, att_w, ssm_w)
    starts = [sum(widths[:i]) for i in range(len(widths))]
    q, k, v, u = (_dot(xb, w_ref[:, c0:c0 + w]) for c0, w in zip(starts, widths))
    return (q * (HEAD_DIM ** -0.5 * LOG2E)).astype(BF16), k, v, u


def _in_proj_kernel(x_ref, g_ref, b_ref, w_ref, *out_refs, apply_ln, att_w, ssm_w):
    x = x_ref[...]
    if apply_ln:
        xo_ref, *out_refs = out_refs
        x = _layer_norm(x, g_ref[...], b_ref[...])
        xo_ref[...] = x
    q_ref, k_ref, v_ref, u_ref = out_refs
    q_ref[...], k_ref[...], v_ref[...], u_ref[...] = _project_qkvu(x, w_ref, att_w, ssm_w)


def _in_proj(x, ln_g, ln_b, w_qkvu, *, layer, tm, att_w, ssm_w):
    n, d = x.shape
    apply_ln = layer == 0
    row = lambda w: pl.BlockSpec((tm, w), lambda i: (i, 0))
    outs = [(att_w, BF16), (att_w, F32), (att_w, F32), (ssm_w, F32)]
    if apply_ln:
        outs = [(d, F32)] + outs
    return pl.pallas_call(
        functools.partial(_in_proj_kernel, apply_ln=apply_ln, att_w=att_w, ssm_w=ssm_w),
        grid=(n // tm,),
        in_specs=[row(d), _resident((1, d)), _resident((1, d)), _of_layer(w_qkvu, layer)],
        out_specs=[row(w) for w, _ in outs],
        out_shape=[jax.ShapeDtypeStruct((n, w), dt) for w, dt in outs],
        compiler_params=pltpu.CompilerParams(dimension_semantics=("parallel",),
                                             vmem_limit_bytes=V7X_VMEM_LIMIT),
        name="in_proj",
    )(x, ln_g, ln_b, w_qkvu)


def _in_proj_prompt_kernel(*refs, first, tm, att_w, ssm_w):
    if first:
        x_ref, meta_ref, g_ref, b_ref, w_ref, xo_ref, q_ref, k_ref, vb_ref, u_ref, vall_ref = refs
    else:
        x_ref, w_ref, _, q_ref, k_ref, vb_ref, u_ref, vall_ref = refs
    n_heads = att_w // HEAD_WIDTH

    def body(x):
        if first:
            x = _layer_norm(x, g_ref[...], b_ref[...])
            xo_ref[...] = x
        q_ref[...], k_ref[...], v, u_ref[...] = _project_qkvu(x, w_ref, att_w, ssm_w)
        vb_ref[...] = v.astype(BF16)
        for h in range(n_heads):
            vall_ref[pl.ds(h, tm, stride=n_heads), :] = v[:, h * HEAD_WIDTH:(h + 1) * HEAD_WIDTH]

    if first:
        @pl.when(pl.program_id(1) == 0)
        def _():
            body(jnp.concatenate([meta_ref[...], x_ref[:tm - N_META, :]], axis=0))

        @pl.when(pl.program_id(1) != 0)
        def _():
            body(x_ref[...])
    else:
        body(x_ref[...])


def _in_proj_prompt(x, meta, ln_g, ln_b, w_qkvu, v_all, *, layer, depth, bsz, t, tm, att_w, ssm_w):
    first = layer == 0
    d = x.shape[-1]
    n = bsz * t
    n_j = t // tm
    n_heads = att_w // HEAD_WIDTH
    row = lambda w: pl.BlockSpec((tm, w), lambda b, j: (b * n_j + j, 0))
    outs = [(att_w, BF16), (att_w, F32), (att_w, BF16), (ssm_w, F32)]
    if first:
        window = pl.BlockSpec(
            (pl.Squeezed(), pl.Element(tm), pl.Element(d)),
            lambda b, j: (b, pl.multiple_of(jnp.maximum(j * tm - N_META, 0), N_META), 0))
        in_specs = [window, _resident(meta.shape), _resident((1, d)), _resident((1, d)),
                    _of_layer(w_qkvu, layer)]
        args = (x, meta, ln_g, ln_b, w_qkvu)
        outs = [(d, F32)] + outs
        aliases = {}
    else:
        in_specs = [row(d), _of_layer(w_qkvu, layer), pl.BlockSpec(memory_space=pl.ANY)]
        args = (x, w_qkvu, v_all)
        aliases = {2: len(outs)}
    vall_shape = (depth, n * n_heads, HEAD_WIDTH)
    return pl.pallas_call(
        functools.partial(_in_proj_prompt_kernel, first=first, tm=tm, att_w=att_w, ssm_w=ssm_w),
        grid=(bsz, n_j),
        in_specs=in_specs,
        out_specs=[row(w) for w, _ in outs]
        + [pl.BlockSpec((None, tm * n_heads, HEAD_WIDTH), lambda b, j: (layer, b * n_j + j, 0))],
        out_shape=[jax.ShapeDtypeStruct((n, w), dt) for w, dt in outs]
        + [jax.ShapeDtypeStruct(vall_shape, F32)],
        input_output_aliases=aliases,
        compiler_params=pltpu.CompilerParams(dimension_semantics=("parallel", "parallel"),
                                             vmem_limit_bytes=V7X_VMEM_LIMIT),
        name="in_proj_prompt",
    )(*args)


def _bias_kernel(rb_ref, rel_ref, o_ref):
    rel = rel_ref[...]
    n = jnp.maximum(rel, 0)
    exact = N_BUCKETS // 2
    large = exact + (jnp.log(jnp.maximum(n, 1).astype(F32) / exact)
                     / math.log(MAX_DISTANCE / exact) * (N_BUCKETS - exact)).astype(jnp.int32)
    bucket = jnp.where(n < exact, n, jnp.minimum(large, N_BUCKETS - 1))
    for h in range(o_ref.shape[0]):
        far = rb_ref[N_BUCKETS - 1, h]
        acc = jnp.zeros(rel.shape, F32)
        for b in range(N_BUCKETS - 1):
            acc = jnp.where(bucket == b, (rb_ref[b, h] - far) * LOG2E, acc)
        o_ref[h] = jnp.where(rel >= 0, acc, -jnp.inf)


def _bias_tile(rel_bias, rel):
    n_heads = rel_bias.shape[1]
    return pl.pallas_call(
        _bias_kernel,
        in_specs=[pl.BlockSpec(memory_space=pltpu.SMEM), pl.BlockSpec(rel.shape, lambda: (0, 0))],
        out_specs=pl.BlockSpec((n_heads,) + rel.shape, lambda: (0, 0, 0)),
        out_shape=jax.ShapeDtypeStruct((n_heads,) + rel.shape, F32),
        name="bias_tile",
    )(rel_bias.astype(F32), rel)


def _bias_tables(rel_bias, tq, dec_seq):
    i32 = jnp.int32
    n_heads = rel_bias.shape[1]
    r = jnp.arange(tq, dtype=i32)[:, None]
    near = _bias_tile(rel_bias, tq + r - jnp.arange(2 * tq, dtype=i32)[None, :])
    mb = _bias_tile(rel_bias, N_META + r - jnp.arange(N_META, dtype=i32)[None, :])
    mr = jnp.arange(N_META, dtype=i32)
    mm = _bias_tile(rel_bias, mr[:, None] - mr[None, :])
    s = jnp.arange(dec_seq, dtype=i32)[:, None]
    kk = jnp.arange(PAGE_SIZE, dtype=i32)[None, :]
    last = _bias_tile(rel_bias, PAGE_SIZE + s - kk)
    new = _bias_tile(rel_bias, jnp.where(kk < dec_seq, s - kk, -1))

    def rows(tile):
        return jnp.broadcast_to(tile[:, None], (n_heads, 2, dec_seq, PAGE_SIZE)).reshape(-1, PAGE_SIZE)

    return near, mb, mm, jnp.stack([rows(last), rows(new)])


def _softmax_pv(blocks):
    m = functools.reduce(jnp.maximum, [jnp.max(s, -1, keepdims=True) for s, _ in blocks])
    acc = 0.0
    for s, v_ones in blocks:
        acc = acc + _dot(jnp.exp2(s - m).astype(BF16), v_ones)
    return acc[:, :HEAD_WIDTH] / acc[:, HEAD_WIDTH:HEAD_WIDTH + 1]


def _sub_norm(o1, o2, lam, g, sub_scale):
    o = o1 - lam * o2
    return o * lax.rsqrt(jnp.mean(o * o, -1, keepdims=True) + EPS) * g * sub_scale


def _attn_prompt_kernel(lam_ref, q_ref, k_ref, v_ref, near_ref, mb_ref, mm_ref, g_ref, *rest,
                        layer, tq, n_q, sub_scale):
    o_ref, kt_ref, k1_ref, k2_ref, v1_ref = rest[-5:]
    lam = lam_ref[layer]
    g = g_ref[...]
    k = k_ref[0]
    t = k.shape[0]
    lane = lax.broadcasted_iota(jnp.int32, k.shape, 1)
    k1_ref[...] = jnp.where(lane < HEAD_DIM, k, 0.0).astype(BF16)
    k2_ref[...] = jnp.where(lane >= HEAD_DIM, k, 0.0).astype(BF16)
    v1_ref[:, :HEAD_WIDTH] = v_ref[0]
    v1_ref[:, HEAD_WIDTH:] = (lane == 0).astype(F32).astype(BF16)

    n_full = t // LANES
    for c in range(n_full):
        kt_ref[:, c * LANES:(c + 1) * LANES] = k_ref[0, c * LANES:(c + 1) * LANES, :].T
    tail = t - n_full * LANES
    if tail:
        k_tail = jnp.concatenate([k_ref[0, n_full * LANES:, :],
                                  jnp.zeros((LANES - tail, HEAD_WIDTH), F32)], axis=0)
        kt_ref[:, n_full * LANES:] = k_tail.T[:, :tail]

    def scores(rows, spans):
        q_t = q_ref[0, rows, :]
        comps = []
        for kc_ref in (k1_ref, k2_ref):
            blocks = []
            for k0, size, bias in spans:
                s = _dot_nt(q_t, kc_ref[k0:k0 + size, :])
                blocks.append((s if bias is None else s + bias, v1_ref[k0:k0 + size, :]))
            comps.append(blocks)
        return comps

    def finish(rows, comps):
        outs = [_softmax_pv(blocks) for blocks in comps]
        o_ref[0, rows, :] = _sub_norm(outs[0], outs[1], lam, g, sub_scale).astype(BF16)

    tiles = [(slice(0, N_META), [(0, N_META, mm_ref[0])])]
    for i in range(n_q):
        r0 = N_META + i * tq
        if i == 0:
            spans = [(0, N_META, mb_ref[0]), (r0, tq, near_ref[0, :, tq:])]
        else:
            spans = [(0, r0 - tq, None), (r0 - tq, 2 * tq, near_ref[0])]
        tiles.append((slice(r0, r0 + tq), spans))
    pending = []
    for rows, spans in tiles:
        pending.append((rows, scores(rows, spans)))
        if len(pending) > ATT_LOOKAHEAD:
            finish(*pending.pop(0))
    for item in pending:
        finish(*item)


def _attn_prompt(lam, q, k, vb, near, mb, mm, g, kt_all, *, layer, depth, sub_scale):
    bsz, t, att_w = q.shape
    n_heads = att_w // HEAD_WIDTH
    tq = near.shape[1]
    n_q = (t - N_META) // tq
    assert n_q * tq + N_META == t and tq > MAX_DISTANCE
    smem = pl.BlockSpec(memory_space=pltpu.SMEM)
    seq = pl.BlockSpec((1, t, HEAD_WIDTH), lambda b, h: (b, 0, h))
    per_head = lambda a: pl.BlockSpec((1,) + a.shape[1:], lambda b, h: (h, 0, 0))
    in_specs = [smem, seq, seq, seq, per_head(near), per_head(mb), per_head(mm),
                _of_layer(g, layer)]
    args = (lam, q, k, vb, near, mb, mm, g)
    aliases = {}
    if kt_all is not None:
        in_specs.append(pl.BlockSpec(memory_space=pl.ANY))
        args += (kt_all,)
        aliases = {len(args) - 1: 1}
    return pl.pallas_call(
        functools.partial(_attn_prompt_kernel, layer=layer, tq=tq, n_q=n_q, sub_scale=sub_scale),
        grid=(bsz, n_heads),
        in_specs=in_specs,
        out_specs=[seq, pl.BlockSpec((None, None, HEAD_WIDTH, t), lambda b, h: (layer, b, h, 0))],
        out_shape=[jax.ShapeDtypeStruct(q.shape, BF16),
                   jax.ShapeDtypeStruct((depth, bsz, att_w, t), F32)],
        input_output_aliases=aliases,
        scratch_shapes=[pltpu.VMEM((t, HEAD_WIDTH), BF16), pltpu.VMEM((t, HEAD_WIDTH), BF16),
                        pltpu.VMEM((t, 2 * HEAD_WIDTH), BF16)],
        compiler_params=pltpu.CompilerParams(dimension_semantics=("parallel", "parallel"),
                                             vmem_limit_bytes=V7X_VMEM_LIMIT),
        name="attn_prompt",
    )(*args)


def _attn_sample_kernel(pt_ref, lam_ref, q_ref, kn_ref, vn_ref, bias_ref, g_ref, *rest,
                        layer, n_pages, n_heads, dec_seq, sub_scale):
    kt_pages = rest[:n_pages]
    v_pages = rest[n_pages:2 * n_pages]
    o_ref, s_ref = rest[2 * n_pages:]
    del pt_ref
    lam = lam_ref[layer]
    att_w = q_ref.shape[2]
    n_rows = 2 * n_heads * dec_seq
    head_rows = 2 * dec_seq
    q = q_ref[0]
    q_rep = jnp.concatenate([q] * (2 * n_heads), axis=0)
    row_blk = lax.broadcasted_iota(jnp.int32, (n_rows, att_w), 0) // dec_seq
    col_blk = lax.broadcasted_iota(jnp.int32, (n_rows, att_w), 1) // HEAD_DIM
    q_exp = jnp.where(row_blk == col_blk, q_rep, 0.0).astype(BF16)

    pad = jnp.zeros((PAGE_SIZE - dec_seq, att_w), F32)
    k_new = jnp.concatenate([kn_ref[0], pad], axis=0).astype(BF16)
    v_new = jnp.concatenate([vn_ref[0], pad], axis=0).astype(BF16)

    m_el = None
    for p in range(n_pages + 1):
        if p < n_pages:
            s = _dot(q_exp, kt_pages[p][...].astype(BF16))
            if p == n_pages - 1:
                s = s + bias_ref[0]
        else:
            s = _dot_nt(q_exp, k_new) + bias_ref[1]
        s_ref[p] = s
        m_el = s if m_el is None else jnp.maximum(m_el, s)
    m = jnp.max(m_el, -1, keepdims=True)

    l_el = jnp.zeros((n_rows, PAGE_SIZE), F32)
    accs = [jnp.zeros((head_rows, HEAD_WIDTH), F32) for _ in range(n_heads)]
    for p in range(n_pages + 1):
        e = jnp.exp2(s_ref[p] - m)
        l_el = l_el + e
        eb = e.astype(BF16)
        for h in range(n_heads):
            if p < n_pages:
                v_h = v_pages[p][pl.ds(h, PAGE_SIZE, stride=n_heads), :].astype(BF16)
            else:
                v_h = v_new[:, h * HEAD_WIDTH:(h + 1) * HEAD_WIDTH]
            accs[h] = accs[h] + _dot(eb[h * head_rows:(h + 1) * head_rows, :], v_h)
    l = jnp.sum(l_el, -1, keepdims=True)

    g = g_ref[...]
    heads = []
    for h in range(n_heads):
        o = accs[h] / l[h * head_rows:(h + 1) * head_rows]
        heads.append(_sub_norm(o[:dec_seq], o[dec_seq:], lam, g, sub_scale))
    o_ref[0] = jnp.concatenate(heads, axis=-1)


def _attn_sample(page_table, lam, q, k_new, v_new, bias, g, cache_kt, cache_v, *, layer, sub_scale):
    dec_b, dec_seq, att_w = q.shape
    n_pages = page_table.shape[1]
    n_heads = att_w // HEAD_WIDTH
    n_rows = 2 * n_heads * dec_seq
    pt_flat = page_table.reshape(-1)
    per_b = pl.BlockSpec((1, dec_seq, att_w), lambda b, pt: (b, 0, 0))

    def page_spec(p):
        return pl.BlockSpec((None, None) + cache_v.shape[2:],
                            lambda b, pt: (layer, pt[b * n_pages + p], 0, 0))

    pages = [page_spec(p) for p in range(n_pages)]
    grid_spec = pltpu.PrefetchScalarGridSpec(
        num_scalar_prefetch=1,
        grid=(dec_b,),
        in_specs=[pl.BlockSpec(memory_space=pltpu.SMEM), per_b, per_b, per_b,
                  pl.BlockSpec(bias.shape, lambda b, pt: (0, 0, 0)),
                  _of_layer(g, layer)] + pages + pages,
        out_specs=per_b,
        scratch_shapes=[pltpu.VMEM((n_pages + 1, n_rows, PAGE_SIZE), F32)],
    )
    return pl.pallas_call(
        functools.partial(_attn_sample_kernel, layer=layer, n_pages=n_pages, n_heads=n_heads,
                          dec_seq=dec_seq, sub_scale=sub_scale),
        grid_spec=grid_spec,
        out_shape=jax.ShapeDtypeStruct(q.shape, F32),
        compiler_params=pltpu.CompilerParams(dimension_semantics=("parallel",),
                                             vmem_limit_bytes=V7X_VMEM_LIMIT),
        name="attn_sample",
    )(pt_flat, lam, q, k_new, v_new, bias, g, *([cache_kt] * n_pages), *([cache_v] * n_pages))


def _ssm_kernel(u_ref, h0r_ref, h0i_ref, lbr_ref, lbi_ref, br_ref, bi_ref, cr_ref, ci_ref, d_ref,
                z_ref, hr_ref, hi_ref, u_tm, z_tm, *x_scratch, nb, tc):
    n_parts = u_tm.shape[0]
    part_s = lbr_ref.shape[1] // n_parts
    n_rows = nb * tc
    x_re, x_im = x_scratch[:n_parts], x_scratch[n_parts:]

    @pl.when(pl.program_id(0) == 0)
    def _():
        hr_ref[...] = h0r_ref[...]
        hi_ref[...] = h0i_ref[...]

    def to_time_major(b, c):
        ub = u_ref[b]
        for p in range(n_parts):
            u_tm[p, pl.ds(b, tc, stride=nb), :] = ub[:, p * LANES:(p + 1) * LANES]
        return c

    lax.fori_loop(0, nb, to_time_major, 0, unroll=(nb == 8))

    def project_in(p):
        up = u_tm[p].astype(BF16)
        x_re[p][...] = _dot(up, br_ref[p])
        x_im[p][...] = _dot(up, bi_ref[p])

    def recur(p):
        gcols = slice(p * part_s, (p + 1) * part_s)
        lr = lbr_ref[:, gcols]
        li = lbi_ref[:, gcols]

        def rows(r0):
            hr = hr_ref[pl.ds(r0, 8), gcols]
            hi = hi_ref[pl.ds(r0, 8), gcols]
            for t in range(tc):
                row = pl.ds(t * nb + r0, 8)
                hr, hi = (lr * hr - li * hi + x_re[p][row, :], lr * hi + li * hr + x_im[p][row, :])
                x_re[p][row, :] = hr
                x_im[p][row, :] = hi
            hr_ref[pl.ds(r0, 8), gcols] = hr
            hi_ref[pl.ds(r0, 8), gcols] = hi

        if nb == 8:
            rows(0)
        else:
            lax.fori_loop(0, nb // 8, lambda rg, c: (rows(pl.multiple_of(rg * 8, 8)), c)[1], 0)

    def project_out(p):
        rc = n_rows if n_rows <= 512 else 256
        for r0 in range(0, n_rows, rc):
            rows = slice(r0, r0 + rc)
            y = (_dot(x_re[p][rows, :].astype(BF16), cr_ref[p])
                 + _dot(x_im[p][rows, :].astype(BF16), ci_ref[p])
                 + d_ref[:, p * LANES:(p + 1) * LANES] * u_tm[p, rows, :])
            z_tm[p, rows, :] = jax.nn.gelu(y)

    for step in range(n_parts + 2):
        if step < n_parts:
            project_in(step)
        if 1 <= step <= n_parts:
            recur(step - 1)
        if step >= 2:
            project_out(step - 2)

    def to_batch_major(b, c):
        z_ref[b] = jnp.concatenate(
            [z_tm[p, pl.ds(b, tc, stride=nb), :] for p in range(n_parts)], axis=-1)
        return c

    lax.fori_loop(0, nb, to_batch_major, 0, unroll=(nb == 8))


def _ssm(u, h0_re, h0_im, tables, *, layer, tc):
    nb, t, ssm_w = u.shape
    n_state = tables[0].shape[-1]
    n_parts = ssm_w // LANES
    assert t % tc == 0 and nb % 8 == 0 and tables[2].shape[1:] == (n_parts, LANES, n_state // n_parts)
    state = pl.BlockSpec((nb, n_state), lambda i: (0, 0))
    seq = pl.BlockSpec((nb, tc, ssm_w), lambda i: (0, i, 0))
    return pl.pallas_call(
        functools.partial(_ssm_kernel, nb=nb, tc=tc),
        grid=(t // tc,),
        in_specs=[seq, state, state] + [_of_layer(a, layer) for a in tables],
        out_specs=[seq, state, state],
        out_shape=[jax.ShapeDtypeStruct(u.shape, F32), jax.ShapeDtypeStruct((nb, n_state), F32),
                   jax.ShapeDtypeStruct((nb, n_state), F32)],
        scratch_shapes=[pltpu.VMEM((n_parts, nb * tc, LANES), F32)] * 2
        + [pltpu.VMEM((nb * tc, n_state // n_parts), F32)] * (2 * n_parts),
        compiler_params=pltpu.CompilerParams(dimension_semantics=("arbitrary",),
                                             vmem_limit_bytes=V7X_VMEM_LIMIT),
        name="ssm",
    )(u, h0_re, h0_im, *tables)


def _merge_kernel(x_ref, on_ref, z_ref, wgate_ref, wau_ref, wglu_ref, wout_ref, g_ref, b_ref,
                  o_ref, *, alpha):
    d = x_ref.shape[1]
    x = x_ref[...]
    xb = x.astype(BF16)
    on = on_ref[...].astype(BF16)
    zb = z_ref[...].astype(BF16)

    def branch_matmuls(c0):
        cols = slice(c0, c0 + MERGE_CHUNK)
        gcols = slice(d + c0, d + c0 + MERGE_CHUNK)
        return (_dot(xb, wgate_ref[:, cols]), _dot(xb, wgate_ref[:, gcols]),
                _dot(on, wau_ref[:, cols]), _dot(zb, wglu_ref[:, cols]), _dot(zb, wglu_ref[:, gcols]))

    def mix_out(c0, parts):
        ga, gb, a_branch, glu_v, glu_g = parts
        b_branch = glu_v * jax.nn.sigmoid(glu_g)
        mix_in = jax.nn.sigmoid(ga) * a_branch + jax.nn.sigmoid(gb) * b_branch
        return _dot(mix_in.astype(BF16), wout_ref[c0:c0 + MERGE_CHUNK, :])

    acc = alpha * x
    pending = None
    for c0 in range(0, d, MERGE_CHUNK):
        parts = branch_matmuls(c0)
        if pending is not None:
            acc = acc + mix_out(*pending)
        pending = (c0, parts)
    acc = acc + mix_out(*pending)
    o_ref[...] = _layer_norm(acc, g_ref[...], b_ref[...])


def _merge(x, o_n, z, w_gate, w_att_up, w_glu, w_out, ln_g, ln_b, *, layer, tm, alpha):
    n, d = x.shape
    row = lambda a: pl.BlockSpec((tm, a.shape[1]), lambda i: (i, 0))
    params = (w_gate, w_att_up, w_glu, w_out, ln_g, ln_b)
    return pl.pallas_call(
        functools.partial(_merge_kernel, alpha=alpha),
        grid=(n // tm,),
        in_specs=[row(x), row(o_n), row(z)] + [_of_layer(a, layer) for a in params],
        out_specs=row(x),
        out_shape=jax.ShapeDtypeStruct(x.shape, F32),
        compiler_params=pltpu.CompilerParams(dimension_semantics=("parallel",),
                                             vmem_limit_bytes=V7X_VMEM_LIMIT),
        name="merge",
    )(x, o_n, z, w_gate, w_att_up, w_glu, w_out, ln_g, ln_b)


def _ffn_kernel(x_ref, wup_ref, wdn_ref, g_ref, b_ref, o_ref, *, alpha, ff_chunk):
    x = x_ref[...]
    xb = x.astype(BF16)
    acc = alpha * x
    for c0 in range(0, wup_ref.shape[1], ff_chunk):
        hid = jnp.maximum(_dot(xb, wup_ref[:, c0:c0 + ff_chunk]), 0.0)
        acc = acc + _dot((hid * hid).astype(BF16), wdn_ref[c0:c0 + ff_chunk, :])
    o_ref[...] = _layer_norm(acc, g_ref[...], b_ref[...])


def _ffn(x, w_up, w_down, ln_g, ln_b, *, layer, tm, alpha):
    n, d = x.shape
    row = pl.BlockSpec((tm, d), lambda i: (i, 0))
    return pl.pallas_call(
        functools.partial(_ffn_kernel, alpha=alpha, ff_chunk=d),
        grid=(n // tm,),
        in_specs=[row] + [_of_layer(a, layer) for a in (w_up, w_down, ln_g, ln_b)],
        out_specs=row,
        out_shape=jax.ShapeDtypeStruct(x.shape, F32),
        compiler_params=pltpu.CompilerParams(dimension_semantics=("parallel",),
                                             vmem_limit_bytes=V7X_VMEM_LIMIT),
        name="ffn",
    )(x, w_up, w_down, ln_g, ln_b)


def _ffn_skip_meta(x, w_up, w_down, ln_g, ln_b, *, layer, tm, alpha):
    bsz, t, d = x.shape
    seq = t - N_META
    assert seq % tm == 0
    window = pl.BlockSpec((pl.Squeezed(), pl.Element(tm), pl.Element(d)),
                          lambda b, j: (b, pl.multiple_of(N_META + j * tm, N_META), 0))
    return pl.pallas_call(
        functools.partial(_ffn_kernel, alpha=alpha, ff_chunk=d),
        grid=(bsz, seq // tm),
        in_specs=[window] + [_of_layer(a, layer) for a in (w_up, w_down, ln_g, ln_b)],
        out_specs=pl.BlockSpec((None, tm, d), lambda b, j: (b, j, 0)),
        out_shape=jax.ShapeDtypeStruct((bsz, seq, d), F32),
        compiler_params=pltpu.CompilerParams(dimension_semantics=("parallel", "parallel"),
                                             vmem_limit_bytes=V7X_VMEM_LIMIT),
        name="ffn_skip_meta",
    )(x, w_up, w_down, ln_g, ln_b)


def _ssm_tables(a_re, a_im, log_dt, b_re, b_im, c_re, c_im, d):
    depth, n_groups, n_p = a_re.shape
    a_re = a_re.astype(F32)
    a_im = a_im.astype(F32)
    dt = jnp.exp(log_dt.astype(F32))[..., None]
    mag = jnp.exp(a_re * dt)
    lb_re = mag * jnp.cos(a_im * dt)
    lb_im = mag * jnp.sin(a_im * dt)
    nr, ni = lb_re - 1.0, lb_im
    den = a_re * a_re + a_im * a_im
    f_re = (nr * a_re + ni * a_im) / den
    f_im = (ni * a_re - nr * a_im) / den
    b_re = b_re.astype(F32)
    b_im = b_im.astype(F32)
    bb_re = f_re[..., None] * b_re - f_im[..., None] * b_im
    bb_im = f_re[..., None] * b_im + f_im[..., None] * b_re
    gpp = LANES // GROUP_CH
    n_parts = n_groups // gpp
    eye = jnp.eye(gpp, dtype=F32)

    def in_mat(bb):
        bb = bb.reshape(depth, n_parts, gpp, n_p, GROUP_CH)
        full = jnp.einsum('lngpc,gh->lngchp', bb, eye)
        return full.reshape(depth, n_parts, LANES, gpp * n_p).astype(BF16)

    def out_mat(cc):
        cc = cc.reshape(depth, n_parts, gpp, GROUP_CH, n_p)
        full = jnp.einsum('lngcp,gh->lngphc', cc, eye)
        return full.reshape(depth, n_parts, gpp * n_p, LANES).astype(BF16)

    sublanes = lambda a: jnp.broadcast_to(a.reshape(depth, 1, -1), (depth, 8, n_groups * n_p))
    return (sublanes(lb_re), sublanes(lb_im), in_mat(bb_re), in_mat(bb_im),
            out_mat(c_re.astype(F32)), out_mat(-c_im.astype(F32)), d.astype(F32).reshape(depth, 1, -1))


def kernel(x_prompt, x_sample, cache_k, cache_v, state_ssm_re, state_ssm_im, page_table, meta_tokens, ln_in_g, ln_in_b, rel_bias, w_in, lambda_q1, lambda_k1, lambda_q2, lambda_k2, subln_g, w_att_up, ssm_a_re, ssm_a_im, ssm_log_dt, ssm_b_re, ssm_b_im, ssm_c_re, ssm_c_im, ssm_d, w_glu, w_out, ln1_g, ln1_b, w_up, w_down, ln2_g, ln2_b):
    bsz, seq, d_model = x_prompt.shape
    dec_b, dec_seq, _ = x_sample.shape
    depth = w_in.shape[0]
    att_w = w_att_up.shape[1]
    ssm_w = w_glu.shape[1]
    n_heads = att_w // HEAD_WIDTH
    n_groups = ssm_w // GROUP_CH
    n_state = n_groups * STATE_DIM
    t = N_META + seq
    past = page_table.shape[1] * PAGE_SIZE
    alpha = (2.0 * depth) ** 0.25
    assert seq % ATT_Q_TILE == 0 and t % SSM_CHUNK == 0 and past >= 2 * PAGE_SIZE

    tm_p = _row_tile(t, 704)
    tm_s = _row_tile(dec_b * dec_seq, 512)

    xp = x_prompt
    kt_all = v_all = None
    xs = x_sample.reshape(dec_b * dec_seq, d_model)
    row = lambda a: a.astype(F32).reshape(1, -1)
    near, mb, mm, bias_s = _bias_tables(rel_bias, ATT_Q_TILE, dec_seq)
    n_pool = cache_k.shape[1]
    cache_kt = jnp.transpose(cache_k, (0, 1, 3, 4, 5, 2)).reshape(depth, n_pool, att_w, PAGE_SIZE)
    cache_v = cache_v.reshape(depth, n_pool, PAGE_SIZE * n_heads, HEAD_WIDTH)
    h0p = jnp.zeros((bsz, n_state), F32)

    lam_init = [0.8 - 0.6 * math.exp(-0.3 * l) for l in range(depth)]
    f32 = lambda a: a.astype(F32)
    lam = (jnp.exp(jnp.sum(f32(lambda_q1) * f32(lambda_k1), -1))
           - jnp.exp(jnp.sum(f32(lambda_q2) * f32(lambda_k2), -1)) + jnp.asarray(lam_init, F32))
    rows = lambda a: a.astype(F32).reshape(depth, 1, -1)
    g_sub = rows(subln_g)
    n_qkvu = 3 * att_w + ssm_w
    w_qkvu_bf = w_in[:, :, :n_qkvu].astype(BF16)
    w_gate_bf = w_in[:, :, n_qkvu:].astype(BF16)
    merge_params = (w_gate_bf, w_att_up.astype(BF16), w_glu.astype(BF16), w_out.astype(BF16),
                    rows(ln1_g), rows(ln1_b))
    ffn_params = (w_up.astype(BF16), w_down.astype(BF16), rows(ln2_g), rows(ln2_b))
    ssm_tables = _ssm_tables(ssm_a_re, ssm_a_im, ssm_log_dt, ssm_b_re, ssm_b_im, ssm_c_re, ssm_c_im,
                             ssm_d)
    state_re = state_ssm_re.reshape(depth, dec_b, n_state).astype(F32)
    state_im = state_ssm_im.reshape(depth, dec_b, n_state).astype(F32)

    outs = {name: [] for name in ("hpr", "hpi", "ks", "vs", "hsr", "hsi")}
    for l in range(depth):
        sub_scale = 1.0 - lam_init[l]

        *x_ln, q, kf, vb, u, v_all = _in_proj_prompt(
            xp, meta_tokens.astype(F32), row(ln_in_g), row(ln_in_b), w_qkvu_bf, v_all, layer=l,
            depth=depth, bsz=bsz, t=t, tm=tm_p, att_w=att_w, ssm_w=ssm_w)
        xp = x_ln[0] if x_ln else xp
        seq3 = lambda a: a.reshape(bsz, t, a.shape[-1])
        o_n, kt_all = _attn_prompt(lam, seq3(q), seq3(kf), seq3(vb), near, mb, mm, g_sub, kt_all,
                                   layer=l, depth=depth, sub_scale=sub_scale)
        z, hpr, hpi = _ssm(seq3(u), h0p, h0p, ssm_tables, layer=l, tc=SSM_CHUNK)
        xp = _merge(xp, o_n.reshape(bsz * t, att_w), z.reshape(bsz * t, ssm_w), *merge_params,
                    layer=l, tm=tm_p, alpha=alpha)
        if l < depth - 1:
            xp = _ffn(xp, *ffn_params, layer=l, tm=tm_p, alpha=alpha)
        else:
            y_prompt = _ffn_skip_meta(xp.reshape(bsz, t, d_model), *ffn_params, layer=l,
                                      tm=_row_tile(seq, 512), alpha=alpha)
        outs["hpr"].append(hpr.reshape(bsz, n_groups, STATE_DIM))
        outs["hpi"].append(hpi.reshape(bsz, n_groups, STATE_DIM))

        *x_ln, q, kf, vf, u = _in_proj(xs, row(ln_in_g), row(ln_in_b), w_qkvu_bf, layer=l, tm=tm_s,
                                       att_w=att_w, ssm_w=ssm_w)
        xs = x_ln[0] if x_ln else xs
        dec3 = lambda a: a.reshape(dec_b, dec_seq, a.shape[-1])
        o_n = _attn_sample(page_table, lam, dec3(q.astype(F32)), dec3(kf), dec3(vf), bias_s, g_sub,
                           cache_kt, cache_v, layer=l, sub_scale=sub_scale)
        z, hsr, hsi = _ssm(dec3(u), state_re[l], state_im[l], ssm_tables, layer=l, tc=dec_seq)
        xs = _merge(xs, o_n.reshape(dec_b * dec_seq, att_w), z.reshape(dec_b * dec_seq, ssm_w),
                    *merge_params, layer=l, tm=tm_s, alpha=alpha)
        xs = _ffn(xs, *ffn_params, layer=l, tm=tm_s, alpha=alpha)
        outs["ks"].append(kf.reshape(dec_b, dec_seq, n_heads, 2, HEAD_DIM))
        outs["vs"].append(vf.reshape(dec_b, dec_seq, n_heads, HEAD_WIDTH))
        outs["hsr"].append(hsr.reshape(dec_b, n_groups, STATE_DIM))
        outs["hsi"].append(hsi.reshape(dec_b, n_groups, STATE_DIM))

    y_sample = xs.reshape(dec_b, dec_seq, d_model)
    st = lambda name: jnp.stack(outs[name])
    k_prompt = jnp.transpose(kt_all.reshape(depth, bsz, n_heads, 2, HEAD_DIM, t), (0, 1, 5, 2, 3, 4))
    v_prompt = v_all.reshape(depth, bsz, t, n_heads, HEAD_WIDTH)
    return (y_prompt, y_sample, k_prompt, v_prompt, st("hpr"), st("hpi"), st("ks"), st("vs"),
            st("hsr"), st("hsi"))
```

```python
import functools
import math

import jax
import jax.numpy as jnp
from jax import lax
from jax.experimental import pallas as pl
from jax.experimental.pallas import tpu as pltpu

F32 = jnp.float32
BF16 = jnp.bfloat16

N_META = 16
HEAD_DIM = 64
HEAD_WIDTH = 2 * HEAD_DIM
GROUP_CH = 16
STATE_DIM = 64
PAGE_SIZE = 128
LANES = 128
N_BUCKETS = 32
MAX_DISTANCE = 128
EPS = 1e-5
LOG2E = math.log2(math.e)

ATT_Q_TILE = 256
MERGE_CHUNK = 256
SSM_CHUNK = 48
V7X_VMEM_BYTES = 64 * 1024 * 1024
V7X_VMEM_LIMIT = V7X_VMEM_BYTES * 7 // 8

assert (N_BUCKETS // 2 + int(math.log((MAX_DISTANCE + 1) / (N_BUCKETS // 2))
                             / math.log(MAX_DISTANCE / (N_BUCKETS // 2)) * (N_BUCKETS // 2))
        >= N_BUCKETS - 1)


def _layer_norm(x, g, b):
    mu = jnp.mean(x, -1, keepdims=True)
    xc = x - mu
    var = jnp.mean(xc * xc, -1, keepdims=True)
    return xc * lax.rsqrt(var + EPS) * g + b


def _dot(a, b):
    return jnp.dot(a, b, preferred_element_type=F32)


def _dot_nt(a, b):
    return lax.dot_general(a, b, (((1,), (1,)), ((), ())), preferred_element_type=F32)


def _resident(shape):
    return pl.BlockSpec(shape, lambda *_: (0,) * len(shape), pipeline_mode=pl.Buffered(1))


def _of_layer(a, layer):
    return pl.BlockSpec((None,) + a.shape[1:], lambda *_: (layer,) + (0,) * (a.ndim - 1),
                        pipeline_mode=pl.Buffered(1))


def _row_tile(n_rows, target):
    best = None
    for t in range(16, min(n_rows, target) + 1, 16):
        if n_rows % t == 0:
            best = t
    assert best is not None, n_rows
    return best


def _project_qkvu(x, w_ref, att_w, ssm_w):
    xb = x.astype(BF16)
    widths = (att_w, att_w, att_w, ssm_w)
    starts = [sum(widths[:i]) for i in range(len(widths))]
    q, k, v, u = (_dot(xb, w_ref[:, c0:c0 + w]) for c0, w in zip(starts, widths))
    return (q * (HEAD_DIM ** -0.5 * LOG2E)).astype(BF16), k, v, u


def _in_proj_kernel(x_ref, g_ref, b_ref, w_ref, *out_refs, apply_ln, att_w, ssm_w):
    x = x_ref[...]
    if apply_ln:
        xo_ref, *out_refs = out_refs
        x = _layer_norm(x, g_ref[...], b_ref[...])
        xo_ref[...] = x
    q_ref, k_ref, v_ref, u_ref = out_refs
    q_ref[...], k_ref[...], v_ref[...], u_ref[...] = _project_qkvu(x, w_ref, att_w, ssm_w)


def _in_proj(x, ln_g, ln_b, w_qkvu, *, layer, tm, att_w, ssm_w):
    n, d = x.shape
    apply_ln = layer == 0
    row = lambda w: pl.BlockSpec((tm, w), lambda i: (i, 0))
    outs = [(att_w, BF16), (att_w, F32), (att_w, F32), (ssm_w, F32)]
    if apply_ln:
        outs = [(d, F32)] + outs
    return pl.pallas_call(
        functools.partial(_in_proj_kernel, apply_ln=apply_ln, att_w=att_w, ssm_w=ssm_w),
        grid=(n // tm,),
        in_specs=[row(d), _resident((1, d)), _resident((1, d)), _of_layer(w_qkvu, layer)],
        out_specs=[row(w) for w, _ in outs],
        out_shape=[jax.ShapeDtypeStruct((n, w), dt) for w, dt in outs],
        compiler_params=pltpu.CompilerParams(dimension_semantics=("parallel",),
                                             vmem_limit_bytes=V7X_VMEM_LIMIT),
        name="in_proj",
    )(x, ln_g, ln_b, w_qkvu)


def _in_proj_prompt_kernel(*refs, first, tm, att_w, ssm_w):
    if first:
        x_ref, meta_ref, g_ref, b_ref, w_ref, xo_ref, q_ref, k_ref, vb_ref, u_ref, vall_ref = refs
    else:
        x_ref, w_ref, _, q_ref, k_ref, vb_ref, u_ref, vall_ref = refs
    n_heads = att_w // HEAD_WIDTH

    def body(x):
        if first:
            x = _layer_norm(x, g_ref[...], b_ref[...])
            xo_ref[...] = x
        q_ref[...], k_ref[...], v, u_ref[...] = _project_qkvu(x, w_ref, att_w, ssm_w)
        vb_ref[...] = v.astype(BF16)
        for h in range(n_heads):
            vall_ref[pl.ds(h, tm, stride=n_heads), :] = v[:, h * HEAD_WIDTH:(h + 1) * HEAD_WIDTH]

    if first:
        @pl.when(pl.program_id(1) == 0)
        def _():
            body(jnp.concatenate([meta_ref[...], x_ref[:tm - N_META, :]], axis=0))

        @pl.when(pl.program_id(1) != 0)
        def _():
            body(x_ref[...])
    else:
        body(x_ref[...])


def _in_proj_prompt(x, meta, ln_g, ln_b, w_qkvu, v_all, *, layer, depth, bsz, t, tm, att_w, ssm_w):
    first = layer == 0
    d = x.shape[-1]
    n = bsz * t
    n_j = t // tm
    n_heads = att_w // HEAD_WIDTH
    row = lambda w: pl.BlockSpec((tm, w), lambda b, j: (b * n_j + j, 0))
    outs = [(att_w, BF16), (att_w, F32), (att_w, BF16), (ssm_w, F32)]
    if first:
        window = pl.BlockSpec(
            (pl.Squeezed(), pl.Element(tm), pl.Element(d)),
            lambda b, j: (b, pl.multiple_of(jnp.maximum(j * tm - N_META, 0), N_META), 0))
        in_specs = [window, _resident(meta.shape), _resident((1, d)), _resident((1, d)),
                    _of_layer(w_qkvu, layer)]
        args = (x, meta, ln_g, ln_b, w_qkvu)
        outs = [(d, F32)] + outs
        aliases = {}
    else:
        in_specs = [row(d), _of_layer(w_qkvu, layer), pl.BlockSpec(memory_space=pl.ANY)]
        args = (x, w_qkvu, v_all)
        aliases = {2: len(outs)}
    vall_shape = (depth, n * n_heads, HEAD_WIDTH)
    return pl.pallas_call(
        functools.partial(_in_proj_prompt_kernel, first=first, tm=tm, att_w=att_w, ssm_w=ssm_w),
        grid=(bsz, n_j),
        in_specs=in_specs,
        out_specs=[row(w) for w, _ in outs]
        + [pl.BlockSpec((None, tm * n_heads, HEAD_WIDTH), lambda b, j: (layer, b * n_j + j, 0))],
        out_shape=[jax.ShapeDtypeStruct((n, w), dt) for w, dt in outs]
        + [jax.ShapeDtypeStruct(vall_shape, F32)],
        input_output_aliases=aliases,
        compiler_params=pltpu.CompilerParams(dimension_semantics=("parallel", "parallel"),
                                             vmem_limit_bytes=V7X_VMEM_LIMIT),
        name="in_proj_prompt",
    )(*args)


def _bias_kernel(rb_ref, rel_ref, o_ref):
    rel = rel_ref[...]
    n = jnp.maximum(rel, 0)
    exact = N_BUCKETS // 2
    large = exact + (jnp.log(jnp.maximum(n, 1).astype(F32) / exact)
                     / math.log(MAX_DISTANCE / exact) * (N_BUCKETS - exact)).astype(jnp.int32)
    bucket = jnp.where(n < exact, n, jnp.minimum(large, N_BUCKETS - 1))
    for h in range(o_ref.shape[0]):
        far = rb_ref[N_BUCKETS - 1, h]
        acc = jnp.zeros(rel.shape, F32)
        for b in range(N_BUCKETS - 1):
            acc = jnp.where(bucket == b, (rb_ref[b, h] - far) * LOG2E, acc)
        o_ref[h] = jnp.where(rel >= 0, acc, -jnp.inf)


def _bias_tile(rel_bias, rel):
    n_heads = rel_bias.shape[1]
    return pl.pallas_call(
        _bias_kernel,
        in_specs=[pl.BlockSpec(memory_space=pltpu.SMEM), pl.BlockSpec(rel.shape, lambda: (0, 0))],
        out_specs=pl.BlockSpec((n_heads,) + rel.shape, lambda: (0, 0, 0)),
        out_shape=jax.ShapeDtypeStruct((n_heads,) + rel.shape, F32),
        name="bias_tile",
    )(rel_bias.astype(F32), rel)


def _bias_tables(rel_bias, tq, dec_seq):
    i32 = jnp.int32
    n_heads = rel_bias.shape[1]
    r = jnp.arange(tq, dtype=i32)[:, None]
    near = _bias_tile(rel_bias, tq + r - jnp.arange(2 * tq, dtype=i32)[None, :])
    mb = _bias_tile(rel_bias, N_META + r - jnp.arange(N_META, dtype=i32)[None, :])
    mr = jnp.arange(N_META, dtype=i32)
    mm = _bias_tile(rel_bias, mr[:, None] - mr[None, :])
    s = jnp.arange(dec_seq, dtype=i32)[:, None]
    kk = jnp.arange(PAGE_SIZE, dtype=i32)[None, :]
    last = _bias_tile(rel_bias, PAGE_SIZE + s - kk)
    new = _bias_tile(rel_bias, jnp.where(kk < dec_seq, s - kk, -1))

    def rows(tile):
        return jnp.broadcast_to(tile[:, None], (n_heads, 2, dec_seq, PAGE_SIZE)).reshape(-1, PAGE_SIZE)

    return near, mb, mm, jnp.stack([rows(last), rows(new)])


def _softmax_pv(blocks):
    m = functools.reduce(jnp.maximum, [jnp.max(s, -1, keepdims=True) for s, _ in blocks])
    acc = 0.0
    for s, v_ones in blocks:
        acc = acc + _dot(jnp.exp2(s - m).astype(BF16), v_ones)
    return acc[:, :HEAD_WIDTH] / acc[:, HEAD_WIDTH:HEAD_WIDTH + 1]


def _sub_norm(o1, o2, lam, g, sub_scale):
    o = o1 - lam * o2
    return o * lax.rsqrt(jnp.mean(o * o, -1, keepdims=True) + EPS) * g * sub_scale


def _sample_scores(q_ref, kn_ref, bias_ref, kt_pages, s_ref, *, n_heads, dec_seq):
    n_pages = len(kt_pages)
    att_w = q_ref.shape[2]
    n_rows = 2 * n_heads * dec_seq
    q_rep = jnp.concatenate([q_ref[0]] * (2 * n_heads), axis=0)
    row_blk = lax.broadcasted_iota(jnp.int32, (n_rows, att_w), 0) // dec_seq
    col_blk = lax.broadcasted_iota(jnp.int32, (n_rows, att_w), 1) // HEAD_DIM
    q_exp = jnp.where(row_blk == col_blk, q_rep, 0.0).astype(BF16)
    pad = jnp.zeros((PAGE_SIZE - dec_seq, att_w), F32)
    k_new = jnp.concatenate([kn_ref[0], pad], axis=0).astype(BF16)
    m_el = None
    for p in range(n_pages + 1):
        if p < n_pages:
            s = _dot(q_exp, kt_pages[p][...].astype(BF16))
            if p == n_pages - 1:
                s = s + bias_ref[0]
        else:
            s = _dot_nt(q_exp, k_new) + bias_ref[1]
        s_ref[p] = s
        m_el = s if m_el is None else jnp.maximum(m_el, s)
    return jnp.max(m_el, -1, keepdims=True)


def _sample_values(m, s_ref, vn_ref, v_pages, lam, g, *, n_heads, dec_seq, sub_scale):
    n_pages = len(v_pages)
    att_w = vn_ref.shape[2]
    head_rows = 2 * dec_seq
    pad = jnp.zeros((PAGE_SIZE - dec_seq, att_w), F32)
    v_new = jnp.concatenate([vn_ref[0], pad], axis=0).astype(BF16)
    l_el = jnp.zeros(s_ref.shape[1:], F32)
    accs = [jnp.zeros((head_rows, HEAD_WIDTH), F32) for _ in range(n_heads)]
    for p in range(n_pages + 1):
        e = jnp.exp2(s_ref[p] - m)
        l_el = l_el + e
        eb = e.astype(BF16)
        for h in range(n_heads):
            if p < n_pages:
                v_h = v_pages[p][pl.ds(h, PAGE_SIZE, stride=n_heads), :].astype(BF16)
            else:
                v_h = v_new[:, h * HEAD_WIDTH:(h + 1) * HEAD_WIDTH]
            accs[h] = accs[h] + _dot(eb[h * head_rows:(h + 1) * head_rows, :], v_h)
    l = jnp.sum(l_el, -1, keepdims=True)
    heads = []
    for h in range(n_heads):
        o = accs[h] / l[h * head_rows:(h + 1) * head_rows]
        heads.append(_sub_norm(o[:dec_seq], o[dec_seq:], lam, g, sub_scale))
    return jnp.concatenate(heads, axis=-1)


def _attn_kernel(pt_ref, lam_ref, q_ref, k_ref, v_ref, near_ref, mb_ref, mm_ref, g_ref,
                 qs_ref, kn_ref, vn_ref, bias_ref, *rest, layer, tq, n_q, n_pages, n_heads, dec_seq,
                 sub_scale):
    del pt_ref
    kt_pages = rest[:n_pages]
    v_pages = rest[n_pages:2 * n_pages]
    o_ref, kt_ref, os_ref, k1_ref, k2_ref, v1_ref, s_ref = rest[-7:]
    pair = pl.program_id(2)
    lam = lam_ref[layer]
    g = g_ref[...]
    t = k_ref.shape[1]

    def scores(rows, spans):
        q_t = q_ref[0, rows, :]
        comps = []
        for kc_ref in (k1_ref, k2_ref):
            blocks = []
            for k0, size, bias in spans:
                s = _dot_nt(q_t, kc_ref[k0:k0 + size, :])
                blocks.append((s if bias is None else s + bias, v1_ref[k0:k0 + size, :]))
            comps.append(blocks)
        return comps

    def finish(rows, comps):
        outs = [_softmax_pv(blocks) for blocks in comps]
        o_ref[0, rows, :] = _sub_norm(outs[0], outs[1], lam, g, sub_scale).astype(BF16)

    @pl.when(pair == 0)
    def _():
        k = k_ref[0]
        lane = lax.broadcasted_iota(jnp.int32, k.shape, 1)
        k1_ref[...] = jnp.where(lane < HEAD_DIM, k, 0.0).astype(BF16)
        k2_ref[...] = jnp.where(lane >= HEAD_DIM, k, 0.0).astype(BF16)
        v1_ref[:, :HEAD_WIDTH] = v_ref[0]
        v1_ref[:, HEAD_WIDTH:] = (lane == 0).astype(F32).astype(BF16)
        n_full = t // LANES
        for c in range(n_full):
            kt_ref[:, c * LANES:(c + 1) * LANES] = k_ref[0, c * LANES:(c + 1) * LANES, :].T
        tail = t - n_full * LANES
        if tail:
            k_tail = jnp.concatenate([k_ref[0, n_full * LANES:, :],
                                      jnp.zeros((LANES - tail, HEAD_WIDTH), F32)], axis=0)
            kt_ref[:, n_full * LANES:] = k_tail.T[:, :tail]
        meta_rows = slice(0, N_META)
        finish(meta_rows, scores(meta_rows, [(0, N_META, mm_ref[0])]))

    def tile(i):
        r0 = N_META + i * tq
        if i == 0:
            spans = [(0, N_META, mb_ref[0]), (r0, tq, near_ref[0, :, tq:])]
        else:
            spans = [(0, r0 - tq, None), (r0 - tq, 2 * tq, near_ref[0])]
        return slice(r0, r0 + tq), spans

    sample = dict(n_heads=n_heads, dec_seq=dec_seq)
    for pr in range(n_q // 2):
        @pl.when(pair == pr)
        def _(pr=pr):
            first, second = tile(pr), tile(n_q - 1 - pr)
            m = _sample_scores(qs_ref, kn_ref, bias_ref, kt_pages, s_ref, **sample)
            comps_first = scores(*first)
            comps_second = scores(*second)
            os_ref[0] = _sample_values(m, s_ref, vn_ref, v_pages, lam, g, sub_scale=sub_scale,
                                       **sample)
            finish(first[0], comps_first)
            finish(second[0], comps_second)


def _attn(page_table, lam, q, k, vb, near, mb, mm, g, kt_all, qs, ks_new, vs_new, bias_s, cache_kt,
          cache_v, *, layer, depth, sub_scale):
    bsz, t, att_w = q.shape
    dec_b, dec_seq, _ = qs.shape
    n_pages = page_table.shape[1]
    n_heads = att_w // HEAD_WIDTH
    tq = near.shape[1]
    n_q = (t - N_META) // tq
    n_pairs = n_q // 2
    assert n_q * tq + N_META == t and tq > MAX_DISTANCE and n_q % 2 == 0
    assert dec_b == bsz * n_heads * n_pairs
    item = lambda b, h, p: (b * n_heads + h) * n_pairs + p
    seq = pl.BlockSpec((1, t, HEAD_WIDTH), lambda b, h, p, pt: (b, 0, h))
    per_head = lambda a: pl.BlockSpec((1,) + a.shape[1:], lambda b, h, p, pt: (h, 0, 0))
    per_item = pl.BlockSpec((1, dec_seq, att_w), lambda b, h, p, pt: (item(b, h, p), 0, 0))

    def page_spec(pg):
        return pl.BlockSpec((None, None) + cache_v.shape[2:],
                            lambda b, h, p, pt: (layer, pt[item(b, h, p) * n_pages + pg], 0, 0))

    pages = [page_spec(pg) for pg in range(n_pages)]
    in_specs = [pl.BlockSpec(memory_space=pltpu.SMEM), seq, seq, seq, per_head(near), per_head(mb),
                per_head(mm), _of_layer(g, layer), per_item, per_item, per_item,
                pl.BlockSpec(bias_s.shape, lambda b, h, p, pt: (0, 0, 0))] + pages + pages
    args = (lam, q, k, vb, near, mb, mm, g, qs, ks_new, vs_new, bias_s,
            *([cache_kt] * n_pages), *([cache_v] * n_pages))
    aliases = {}
    if kt_all is not None:
        in_specs.append(pl.BlockSpec(memory_space=pl.ANY))
        args += (kt_all,)
        aliases = {len(args): 1}
    grid_spec = pltpu.PrefetchScalarGridSpec(
        num_scalar_prefetch=1,
        grid=(bsz, n_heads, n_pairs),
        in_specs=in_specs,
        out_specs=[seq, pl.BlockSpec((None, None, HEAD_WIDTH, t),
                                     lambda b, h, p, pt: (layer, b, h, 0)), per_item],
        scratch_shapes=[pltpu.VMEM((t, HEAD_WIDTH), BF16), pltpu.VMEM((t, HEAD_WIDTH), BF16),
                        pltpu.VMEM((t, 2 * HEAD_WIDTH), BF16),
                        pltpu.VMEM((n_pages + 1, 2 * n_heads * dec_seq, PAGE_SIZE), F32)],
    )
    return pl.pallas_call(
        functools.partial(_attn_kernel, layer=layer, tq=tq, n_q=n_q, n_pages=n_pages,
                          n_heads=n_heads, dec_seq=dec_seq, sub_scale=sub_scale),
        grid_spec=grid_spec,
        out_shape=[jax.ShapeDtypeStruct(q.shape, BF16),
                   jax.ShapeDtypeStruct((depth, bsz, att_w, t), F32),
                   jax.ShapeDtypeStruct(qs.shape, F32)],
        input_output_aliases=aliases,
        compiler_params=pltpu.CompilerParams(
            dimension_semantics=("parallel", "parallel", "arbitrary"),
            vmem_limit_bytes=V7X_VMEM_LIMIT),
        name="attn",
    )(page_table.reshape(-1), *args)


def _ssm_kernel(u_ref, h0r_ref, h0i_ref, lbr_ref, lbi_ref, br_ref, bi_ref, cr_ref, ci_ref, d_ref,
                z_ref, hr_ref, hi_ref, u_tm, z_tm, *x_scratch, nb, tc):
    n_parts = u_tm.shape[0]
    part_s = lbr_ref.shape[1] // n_parts
    n_rows = nb * tc
    x_re, x_im = x_scratch[:n_parts], x_scratch[n_parts:]

    @pl.when(pl.program_id(0) == 0)
    def _():
        hr_ref[...] = h0r_ref[...]
        hi_ref[...] = h0i_ref[...]

    def to_time_major(b, c):
        ub = u_ref[b]
        for p in range(n_parts):
            u_tm[p, pl.ds(b, tc, stride=nb), :] = ub[:, p * LANES:(p + 1) * LANES]
        return c

    lax.fori_loop(0, nb, to_time_major, 0, unroll=(nb == 8))

    def project_in(p):
        up = u_tm[p].astype(BF16)
        x_re[p][...] = _dot(up, br_ref[p])
        x_im[p][...] = _dot(up, bi_ref[p])

    def recur(p):
        gcols = slice(p * part_s, (p + 1) * part_s)
        lr = lbr_ref[:, gcols]
        li = lbi_ref[:, gcols]

        def rows(r0):
            hr = hr_ref[pl.ds(r0, 8), gcols]
            hi = hi_ref[pl.ds(r0, 8), gcols]
            for t in range(tc):
                row = pl.ds(t * nb + r0, 8)
                hr, hi = (lr * hr - li * hi + x_re[p][row, :], lr * hi + li * hr + x_im[p][row, :])
                x_re[p][row, :] = hr
                x_im[p][row, :] = hi
            hr_ref[pl.ds(r0, 8), gcols] = hr
            hi_ref[pl.ds(r0, 8), gcols] = hi

        if nb == 8:
            rows(0)
        else:
            lax.fori_loop(0, nb // 8, lambda rg, c: (rows(pl.multiple_of(rg * 8, 8)), c)[1], 0)

    def project_out(p):
        rc = n_rows if n_rows <= 512 else 256
        for r0 in range(0, n_rows, rc):
            rows = slice(r0, r0 + rc)
            y = (_dot(x_re[p][rows, :].astype(BF16), cr_ref[p])
                 + _dot(x_im[p][rows, :].astype(BF16), ci_ref[p])
                 + d_ref[:, p * LANES:(p + 1) * LANES] * u_tm[p, rows, :])
            z_tm[p, rows, :] = jax.nn.gelu(y)

    for step in range(n_parts + 2):
        if step < n_parts:
            project_in(step)
        if 1 <= step <= n_parts:
            recur(step - 1)
        if step >= 2:
            project_out(step - 2)

    def to_batch_major(b, c):
        z_ref[b] = jnp.concatenate(
            [z_tm[p, pl.ds(b, tc, stride=nb), :] for p in range(n_parts)], axis=-1)
        return c

    lax.fori_loop(0, nb, to_batch_major, 0, unroll=(nb == 8))


def _ssm(u, h0_re, h0_im, tables, *, layer, tc):
    nb, t, ssm_w = u.shape
    n_state = tables[0].shape[-1]
    n_parts = ssm_w // LANES
    assert t % tc == 0 and nb % 8 == 0 and tables[2].shape[1:] == (n_parts, LANES, n_state // n_parts)
    state = pl.BlockSpec((nb, n_state), lambda i: (0, 0))
    seq = pl.BlockSpec((nb, tc, ssm_w), lambda i: (0, i, 0))
    return pl.pallas_call(
        functools.partial(_ssm_kernel, nb=nb, tc=tc),
        grid=(t // tc,),
        in_specs=[seq, state, state] + [_of_layer(a, layer) for a in tables],
        out_specs=[seq, state, state],
        out_shape=[jax.ShapeDtypeStruct(u.shape, F32), jax.ShapeDtypeStruct((nb, n_state), F32),
                   jax.ShapeDtypeStruct((nb, n_state), F32)],
        scratch_shapes=[pltpu.VMEM((n_parts, nb * tc, LANES), F32)] * 2
        + [pltpu.VMEM((nb * tc, n_state // n_parts), F32)] * (2 * n_parts),
        compiler_params=pltpu.CompilerParams(dimension_semantics=("arbitrary",),
                                             vmem_limit_bytes=V7X_VMEM_LIMIT),
        name="ssm",
    )(u, h0_re, h0_im, *tables)


def _merge_kernel(x_ref, on_ref, z_ref, wgate_ref, wau_ref, wglu_ref, wout_ref, g_ref, b_ref,
                  o_ref, *, alpha):
    d = x_ref.shape[1]
    x = x_ref[...]
    xb = x.astype(BF16)
    on = on_ref[...].astype(BF16)
    zb = z_ref[...].astype(BF16)

    def branch_matmuls(c0):
        cols = slice(c0, c0 + MERGE_CHUNK)
        gcols = slice(d + c0, d + c0 + MERGE_CHUNK)
        return (_dot(xb, wgate_ref[:, cols]), _dot(xb, wgate_ref[:, gcols]),
                _dot(on, wau_ref[:, cols]), _dot(zb, wglu_ref[:, cols]), _dot(zb, wglu_ref[:, gcols]))

    def mix_out(c0, parts):
        ga, gb, a_branch, glu_v, glu_g = parts
        b_branch = glu_v * jax.nn.sigmoid(glu_g)
        mix_in = jax.nn.sigmoid(ga) * a_branch + jax.nn.sigmoid(gb) * b_branch
        return _dot(mix_in.astype(BF16), wout_ref[c0:c0 + MERGE_CHUNK, :])

    acc = alpha * x
    pending = None
    for c0 in range(0, d, MERGE_CHUNK):
        parts = branch_matmuls(c0)
        if pending is not None:
            acc = acc + mix_out(*pending)
        pending = (c0, parts)
    acc = acc + mix_out(*pending)
    o_ref[...] = _layer_norm(acc, g_ref[...], b_ref[...])


def _merge(x, o_n, z, w_gate, w_att_up, w_glu, w_out, ln_g, ln_b, *, layer, tm, alpha):
    n, d = x.shape
    row = lambda a: pl.BlockSpec((tm, a.shape[1]), lambda i: (i, 0))
    params = (w_gate, w_att_up, w_glu, w_out, ln_g, ln_b)
    return pl.pallas_call(
        functools.partial(_merge_kernel, alpha=alpha),
        grid=(n // tm,),
        in_specs=[row(x), row(o_n), row(z)] + [_of_layer(a, layer) for a in params],
        out_specs=row(x),
        out_shape=jax.ShapeDtypeStruct(x.shape, F32),
        compiler_params=pltpu.CompilerParams(dimension_semantics=("parallel",),
                                             vmem_limit_bytes=V7X_VMEM_LIMIT),
        name="merge",
    )(x, o_n, z, w_gate, w_att_up, w_glu, w_out, ln_g, ln_b)


def _ffn_kernel(x_ref, wup_ref, wdn_ref, g_ref, b_ref, o_ref, *, alpha, ff_chunk):
    x = x_ref[...]
    xb = x.astype(BF16)
    acc = alpha * x
    for c0 in range(0, wup_ref.shape[1], ff_chunk):
        hid = jnp.maximum(_dot(xb, wup_ref[:, c0:c0 + ff_chunk]), 0.0)
        acc = acc + _dot((hid * hid).astype(BF16), wdn_ref[c0:c0 + ff_chunk, :])
    o_ref[...] = _layer_norm(acc, g_ref[...], b_ref[...])


def _ffn(x, w_up, w_down, ln_g, ln_b, *, layer, tm, alpha):
    n, d = x.shape
    row = pl.BlockSpec((tm, d), lambda i: (i, 0))
    return pl.pallas_call(
        functools.partial(_ffn_kernel, alpha=alpha, ff_chunk=d),
        grid=(n // tm,),
        in_specs=[row] + [_of_layer(a, layer) for a in (w_up, w_down, ln_g, ln_b)],
        out_specs=row,
        out_shape=jax.ShapeDtypeStruct(x.shape, F32),
        compiler_params=pltpu.CompilerParams(dimension_semantics=("parallel",),
                                             vmem_limit_bytes=V7X_VMEM_LIMIT),
        name="ffn",
    )(x, w_up, w_down, ln_g, ln_b)


def _ffn_skip_meta(x, w_up, w_down, ln_g, ln_b, *, layer, tm, alpha):
    bsz, t, d = x.shape
    seq = t - N_META
    assert seq % tm == 0
    window = pl.BlockSpec((pl.Squeezed(), pl.Element(tm), pl.Element(d)),
                          lambda b, j: (b, pl.multiple_of(N_META + j * tm, N_META), 0))
    return pl.pallas_call(
        functools.partial(_ffn_kernel, alpha=alpha, ff_chunk=d),
        grid=(bsz, seq // tm),
        in_specs=[window] + [_of_layer(a, layer) for a in (w_up, w_down, ln_g, ln_b)],
        out_specs=pl.BlockSpec((None, tm, d), lambda b, j: (b, j, 0)),
        out_shape=jax.ShapeDtypeStruct((bsz, seq, d), F32),
        compiler_params=pltpu.CompilerParams(dimension_semantics=("parallel", "parallel"),
                                             vmem_limit_bytes=V7X_VMEM_LIMIT),
        name="ffn_skip_meta",
    )(x, w_up, w_down, ln_g, ln_b)


def _ssm_tables(a_re, a_im, log_dt, b_re, b_im, c_re, c_im, d):
    depth, n_groups, n_p = a_re.shape
    a_re = a_re.astype(F32)
    a_im = a_im.astype(F32)
    dt = jnp.exp(log_dt.astype(F32))[..., None]
    mag = jnp.exp(a_re * dt)
    lb_re = mag * jnp.cos(a_im * dt)
    lb_im = mag * jnp.sin(a_im * dt)
    nr, ni = lb_re - 1.0, lb_im
    den = a_re * a_re + a_im * a_im
    f_re = (nr * a_re + ni * a_im) / den
    f_im = (ni * a_re - nr * a_im) / den
    b_re = b_re.astype(F32)
    b_im = b_im.astype(F32)
    bb_re = f_re[..., None] * b_re - f_im[..., None] * b_im
    bb_im = f_re[..., None] * b_im + f_im[..., None] * b_re
    gpp = LANES // GROUP_CH
    n_parts = n_groups // gpp
    eye = jnp.eye(gpp, dtype=F32)

    def in_mat(bb):
        bb = bb.reshape(depth, n_parts, gpp, n_p, GROUP_CH)
        full = jnp.einsum('lngpc,gh->lngchp', bb, eye)
        return full.reshape(depth, n_parts, LANES, gpp * n_p).astype(BF16)

    def out_mat(cc):
        cc = cc.reshape(depth, n_parts, gpp, GROUP_CH, n_p)
        full = jnp.einsum('lngcp,gh->lngphc', cc, eye)
        return full.reshape(depth, n_parts, gpp * n_p, LANES).astype(BF16)

    sublanes = lambda a: jnp.broadcast_to(a.reshape(depth, 1, -1), (depth, 8, n_groups * n_p))
    return (sublanes(lb_re), sublanes(lb_im), in_mat(bb_re), in_mat(bb_im),
            out_mat(c_re.astype(F32)), out_mat(-c_im.astype(F32)), d.astype(F32).reshape(depth, 1, -1))


def kernel(x_prompt, x_sample, cache_k, cache_v, state_ssm_re, state_ssm_im, page_table, meta_tokens, ln_in_g, ln_in_b, rel_bias, w_in, lambda_q1, lambda_k1, lambda_q2, lambda_k2, subln_g, w_att_up, ssm_a_re, ssm_a_im, ssm_log_dt, ssm_b_re, ssm_b_im, ssm_c_re, ssm_c_im, ssm_d, w_glu, w_out, ln1_g, ln1_b, w_up, w_down, ln2_g, ln2_b):
    bsz, seq, d_model = x_prompt.shape
    dec_b, dec_seq, _ = x_sample.shape
    depth = w_in.shape[0]
    att_w = w_att_up.shape[1]
    ssm_w = w_glu.shape[1]
    n_heads = att_w // HEAD_WIDTH
    n_groups = ssm_w // GROUP_CH
    n_state = n_groups * STATE_DIM
    t = N_META + seq
    past = page_table.shape[1] * PAGE_SIZE
    alpha = (2.0 * depth) ** 0.25
    assert seq % ATT_Q_TILE == 0 and t % SSM_CHUNK == 0 and past >= 2 * PAGE_SIZE

    tm_p = _row_tile(t, 704)
    tm_s = _row_tile(dec_b * dec_seq, 512)

    xp = x_prompt
    kt_all = v_all = None
    xs = x_sample.reshape(dec_b * dec_seq, d_model)
    row = lambda a: a.astype(F32).reshape(1, -1)
    near, mb, mm, bias_s = _bias_tables(rel_bias, ATT_Q_TILE, dec_seq)
    n_pool = cache_k.shape[1]
    cache_kt = jnp.transpose(cache_k, (0, 1, 3, 4, 5, 2)).reshape(depth, n_pool, att_w, PAGE_SIZE)
    cache_v = cache_v.reshape(depth, n_pool, PAGE_SIZE * n_heads, HEAD_WIDTH)
    h0p = jnp.zeros((bsz, n_state), F32)

    lam_init = [0.8 - 0.6 * math.exp(-0.3 * l) for l in range(depth)]
    f32 = lambda a: a.astype(F32)
    lam = (jnp.exp(jnp.sum(f32(lambda_q1) * f32(lambda_k1), -1))
           - jnp.exp(jnp.sum(f32(lambda_q2) * f32(lambda_k2), -1)) + jnp.asarray(lam_init, F32))
    rows = lambda a: a.astype(F32).reshape(depth, 1, -1)
    g_sub = rows(subln_g)
    n_qkvu = 3 * att_w + ssm_w
    w_qkvu_bf = w_in[:, :, :n_qkvu].astype(BF16)
    w_gate_bf = w_in[:, :, n_qkvu:].astype(BF16)
    merge_params = (w_gate_bf, w_att_up.astype(BF16), w_glu.astype(BF16), w_out.astype(BF16),
                    rows(ln1_g), rows(ln1_b))
    ffn_params = (w_up.astype(BF16), w_down.astype(BF16), rows(ln2_g), rows(ln2_b))
    ssm_tables = _ssm_tables(ssm_a_re, ssm_a_im, ssm_log_dt, ssm_b_re, ssm_b_im, ssm_c_re, ssm_c_im,
                             ssm_d)
    state_re = state_ssm_re.reshape(depth, dec_b, n_state).astype(F32)
    state_im = state_ssm_im.reshape(depth, dec_b, n_state).astype(F32)

    outs = {name: [] for name in ("hpr", "hpi", "ks", "vs", "hsr", "hsi")}
    for l in range(depth):
        sub_scale = 1.0 - lam_init[l]

        *x_ln, q, kf, vb, u, v_all = _in_proj_prompt(
            xp, meta_tokens.astype(F32), row(ln_in_g), row(ln_in_b), w_qkvu_bf, v_all, layer=l,
            depth=depth, bsz=bsz, t=t, tm=tm_p, att_w=att_w, ssm_w=ssm_w)
        xp = x_ln[0] if x_ln else xp
        *x_ln, qs, ks_new, vs_new, us = _in_proj(xs, row(ln_in_g), row(ln_in_b), w_qkvu_bf, layer=l,
                                                 tm=tm_s, att_w=att_w, ssm_w=ssm_w)
        xs = x_ln[0] if x_ln else xs
        seq3 = lambda a: a.reshape(bsz, t, a.shape[-1])
        dec3 = lambda a: a.reshape(dec_b, dec_seq, a.shape[-1])
        o_n, kt_all, os_n = _attn(page_table, lam, seq3(q), seq3(kf), seq3(vb), near, mb, mm, g_sub,
                                  kt_all, dec3(qs.astype(F32)), dec3(ks_new), dec3(vs_new), bias_s,
                                  cache_kt, cache_v, layer=l, depth=depth, sub_scale=sub_scale)

        z, hpr, hpi = _ssm(seq3(u), h0p, h0p, ssm_tables, layer=l, tc=SSM_CHUNK)
        xp = _merge(xp, o_n.reshape(bsz * t, att_w), z.reshape(bsz * t, ssm_w), *merge_params,
                    layer=l, tm=tm_p, alpha=alpha)
        if l < depth - 1:
            xp = _ffn(xp, *ffn_params, layer=l, tm=tm_p, alpha=alpha)
        else:
            y_prompt = _ffn_skip_meta(xp.reshape(bsz, t, d_model), *ffn_params, layer=l,
                                      tm=_row_tile(seq, 512), alpha=alpha)
        outs["hpr"].append(hpr.reshape(bsz, n_groups, STATE_DIM))
        outs["hpi"].append(hpi.reshape(bsz, n_groups, STATE_DIM))

        z, hsr, hsi = _ssm(dec3(us), state_re[l], state_im[l], ssm_tables, layer=l, tc=dec_seq)
        xs = _merge(xs, os_n.reshape(dec_b * dec_seq, att_w), z.reshape(dec_b * dec_seq, ssm_w),
                    *merge_params, layer=l, tm=tm_s, alpha=alpha)
        xs = _ffn(xs, *ffn_params, layer=l, tm=tm_s, alpha=alpha)
        outs["ks"].append(ks_new.reshape(dec_b, dec_seq, n_heads, 2, HEAD_DIM))
        outs["vs"].append(vs_new.reshape(dec_b, dec_seq, n_heads, HEAD_WIDTH))
        outs["hsr"].append(hsr.reshape(dec_b, n_groups, STATE_DIM))
        outs["hsi"].append(hsi.reshape(dec_b, n_groups, STATE_DIM))

    y_sample = xs.reshape(dec_b, dec_seq, d_model)
    st = lambda name: jnp.stack(outs[name])
    k_prompt = jnp.transpose(kt_all.reshape(depth, bsz, n_heads, 2, HEAD_DIM, t), (0, 1, 5, 2, 3, 4))
    v_prompt = v_all.reshape(depth, bsz, t, n_heads, HEAD_WIDTH)
    return (y_prompt, y_sample, k_prompt, v_prompt, st("hpr"), st("hpi"), st("ks"), st("vs"),
            st("hsr"), st("hsi"))
```

```python
import functools
import math

import jax
import jax.numpy as jnp
from jax import lax
from jax.experimental import pallas as pl
from jax.experimental.pallas import tpu as pltpu

F32 = jnp.float32
BF16 = jnp.bfloat16

N_META = 16
HEAD_DIM = 64
HEAD_WIDTH = 2 * HEAD_DIM
GROUP_CH = 16
STATE_DIM = 64
PAGE_SIZE = 128
LANES = 128
N_BUCKETS = 32
MAX_DISTANCE = 128
EPS = 1e-5
LOG2E = math.log2(math.e)

ATT_Q_TILE = 256
MERGE_CHUNK = 256
SSM_CHUNK = 48
V7X_VMEM_BYTES = 64 * 1024 * 1024
V7X_VMEM_LIMIT = V7X_VMEM_BYTES * 7 // 8

assert (N_BUCKETS // 2 + int(math.log((MAX_DISTANCE + 1) / (N_BUCKETS // 2))
                             / math.log(MAX_DISTANCE / (N_BUCKETS // 2)) * (N_BUCKETS // 2))
        >= N_BUCKETS - 1)


def _layer_norm(x, g, b):
    mu = jnp.mean(x, -1, keepdims=True)
    xc = x - mu
    var = jnp.mean(xc * xc, -1, keepdims=True)
    return xc * lax.rsqrt(var + EPS) * g + b


def _dot(a, b):
    return jnp.dot(a, b, preferred_element_type=F32)


def _dot_nt(a, b):
    return lax.dot_general(a, b, (((1,), (1,)), ((), ())), preferred_element_type=F32)


def _resident(shape):
    return pl.BlockSpec(shape, lambda *_: (0,) * len(shape), pipeline_mode=pl.Buffered(1))


def _of_layer(a, layer):
    return pl.BlockSpec((None,) + a.shape[1:], lambda *_: (layer,) + (0,) * (a.ndim - 1),
                        pipeline_mode=pl.Buffered(1))


def _row_tile(n_rows, target):
    best = None
    for t in range(16, min(n_rows, target) + 1, 16):
        if n_rows % t == 0:
            best = t
    assert best is not None, n_rows
    return best


def _project_qkvu(x, w_ref, att_w, ssm_w):
    xb = x.astype(BF16)
    widths = (att_w, att_w, att_w, ssm_w)
    starts = [sum(widths[:i]) for i in range(len(widths))]
    q, k, v, u = (_dot(xb, w_ref[:, c0:c0 + w]) for c0, w in zip(starts, widths))
    return (q * (HEAD_DIM ** -0.5 * LOG2E)).astype(BF16), k, v, u


def _in_proj_kernel(x_ref, g_ref, b_ref, w_ref, *out_refs, apply_ln, att_w, ssm_w):
    x = x_ref[...]
    if apply_ln:
        xo_ref, *out_refs = out_refs
        x = _layer_norm(x, g_ref[...], b_ref[...])
        xo_ref[...] = x
    q_ref, k_ref, v_ref, u_ref = out_refs
    q, k_ref[...], v_ref[...], u_ref[...] = _project_qkvu(x, w_ref, att_w, ssm_w)
    q_ref[...] = q.astype(F32)


def _in_proj(x, ln_g, ln_b, w_qkvu, *, layer, tm, att_w, ssm_w):
    n, d = x.shape
    apply_ln = layer == 0
    row = lambda w: pl.BlockSpec((tm, w), lambda i: (i, 0))
    outs = [(att_w, F32), (att_w, F32), (att_w, F32), (ssm_w, F32)]
    if apply_ln:
        outs = [(d, F32)] + outs
    return pl.pallas_call(
        functools.partial(_in_proj_kernel, apply_ln=apply_ln, att_w=att_w, ssm_w=ssm_w),
        grid=(n // tm,),
        in_specs=[row(d), _resident((1, d)), _resident((1, d)), _of_layer(w_qkvu, layer)],
        out_specs=[row(w) for w, _ in outs],
        out_shape=[jax.ShapeDtypeStruct((n, w), dt) for w, dt in outs],
        compiler_params=pltpu.CompilerParams(dimension_semantics=("parallel",),
                                             vmem_limit_bytes=V7X_VMEM_LIMIT),
        name="in_proj",
    )(x, ln_g, ln_b, w_qkvu)


def _in_proj_prompt_kernel(*refs, first, tm, att_w, ssm_w):
    if first:
        x_ref, meta_ref, g_ref, b_ref, w_ref, xo_ref, q_ref, k_ref, vb_ref, u_ref, vall_ref = refs
    else:
        x_ref, w_ref, _, q_ref, k_ref, vb_ref, u_ref, vall_ref = refs
    n_heads = att_w // HEAD_WIDTH

    def body(x):
        if first:
            x = _layer_norm(x, g_ref[...], b_ref[...])
            xo_ref[...] = x
        q_ref[...], k_ref[...], v, u_ref[...] = _project_qkvu(x, w_ref, att_w, ssm_w)
        vb_ref[...] = v.astype(BF16)
        for h in range(n_heads):
            vall_ref[pl.ds(h, tm, stride=n_heads), :] = v[:, h * HEAD_WIDTH:(h + 1) * HEAD_WIDTH]

    if first:
        @pl.when(pl.program_id(1) == 0)
        def _():
            body(jnp.concatenate([meta_ref[...], x_ref[:tm - N_META, :]], axis=0))

        @pl.when(pl.program_id(1) != 0)
        def _():
            body(x_ref[...])
    else:
        body(x_ref[...])


def _in_proj_prompt(x, meta, ln_g, ln_b, w_qkvu, v_all, *, layer, depth, bsz, t, tm, att_w, ssm_w):
    first = layer == 0
    d = x.shape[-1]
    n = bsz * t
    n_j = t // tm
    n_heads = att_w // HEAD_WIDTH
    row = lambda w: pl.BlockSpec((tm, w), lambda b, j: (b * n_j + j, 0))
    outs = [(att_w, BF16), (att_w, F32), (att_w, BF16), (ssm_w, F32)]
    if first:
        window = pl.BlockSpec(
            (pl.Squeezed(), pl.Element(tm), pl.Element(d)),
            lambda b, j: (b, pl.multiple_of(jnp.maximum(j * tm - N_META, 0), N_META), 0))
        in_specs = [window, _resident(meta.shape), _resident((1, d)), _resident((1, d)),
                    _of_layer(w_qkvu, layer)]
        args = (x, meta, ln_g, ln_b, w_qkvu)
        outs = [(d, F32)] + outs
        aliases = {}
    else:
        in_specs = [row(d), _of_layer(w_qkvu, layer), pl.BlockSpec(memory_space=pl.ANY)]
        args = (x, w_qkvu, v_all)
        aliases = {2: len(outs)}
    vall_shape = (depth, n * n_heads, HEAD_WIDTH)
    return pl.pallas_call(
        functools.partial(_in_proj_prompt_kernel, first=first, tm=tm, att_w=att_w, ssm_w=ssm_w),
        grid=(bsz, n_j),
        in_specs=in_specs,
        out_specs=[row(w) for w, _ in outs]
        + [pl.BlockSpec((None, tm * n_heads, HEAD_WIDTH), lambda b, j: (layer, b * n_j + j, 0))],
        out_shape=[jax.ShapeDtypeStruct((n, w), dt) for w, dt in outs]
        + [jax.ShapeDtypeStruct(vall_shape, F32)],
        input_output_aliases=aliases,
        compiler_params=pltpu.CompilerParams(dimension_semantics=("parallel", "parallel"),
                                             vmem_limit_bytes=V7X_VMEM_LIMIT),
        name="in_proj_prompt",
    )(*args)


def _bias_kernel(rb_ref, rel_ref, o_ref):
    rel = rel_ref[...]
    n = jnp.maximum(rel, 0)
    exact = N_BUCKETS // 2
    large = exact + (jnp.log(jnp.maximum(n, 1).astype(F32) / exact)
                     / math.log(MAX_DISTANCE / exact) * (N_BUCKETS - exact)).astype(jnp.int32)
    bucket = jnp.where(n < exact, n, jnp.minimum(large, N_BUCKETS - 1))
    for h in range(o_ref.shape[0]):
        far = rb_ref[N_BUCKETS - 1, h]
        acc = jnp.zeros(rel.shape, F32)
        for b in range(N_BUCKETS - 1):
            acc = jnp.where(bucket == b, (rb_ref[b, h] - far) * LOG2E, acc)
        o_ref[h] = jnp.where(rel >= 0, acc, -jnp.inf)


def _bias_tile(rel_bias, rel):
    n_heads = rel_bias.shape[1]
    return pl.pallas_call(
        _bias_kernel,
        in_specs=[pl.BlockSpec(memory_space=pltpu.SMEM), pl.BlockSpec(rel.shape, lambda: (0, 0))],
        out_specs=pl.BlockSpec((n_heads,) + rel.shape, lambda: (0, 0, 0)),
        out_shape=jax.ShapeDtypeStruct((n_heads,) + rel.shape, F32),
        name="bias_tile",
    )(rel_bias.astype(F32), rel)


def _bias_tables(rel_bias, tq, dec_seq):
    i32 = jnp.int32
    n_heads = rel_bias.shape[1]
    r = jnp.arange(tq, dtype=i32)[:, None]
    near = _bias_tile(rel_bias, tq + r - jnp.arange(2 * tq, dtype=i32)[None, :])
    mb = _bias_tile(rel_bias, N_META + r - jnp.arange(N_META, dtype=i32)[None, :])
    mr = jnp.arange(N_META, dtype=i32)
    mm = _bias_tile(rel_bias, mr[:, None] - mr[None, :])
    s = jnp.arange(dec_seq, dtype=i32)[:, None]
    kk = jnp.arange(PAGE_SIZE, dtype=i32)[None, :]
    last = _bias_tile(rel_bias, PAGE_SIZE + s - kk)
    new = _bias_tile(rel_bias, jnp.where(kk < dec_seq, s - kk, -1))

    def rows(tile):
        return jnp.broadcast_to(tile[:, None], (n_heads, 2, dec_seq, PAGE_SIZE)).reshape(-1, PAGE_SIZE)

    return near, mb, mm, jnp.stack([rows(last), rows(new)])


def _softmax_pv(blocks):
    m = functools.reduce(jnp.maximum, [jnp.max(s, -1, keepdims=True) for s, _ in blocks])
    acc = 0.0
    for s, v_ones in blocks:
        acc = acc + _dot(jnp.exp2(s - m).astype(BF16), v_ones)
    return acc[:, :HEAD_WIDTH] / acc[:, HEAD_WIDTH:HEAD_WIDTH + 1]


def _sub_norm(o1, o2, lam, g, sub_scale):
    o = o1 - lam * o2
    return o * lax.rsqrt(jnp.mean(o * o, -1, keepdims=True) + EPS) * g * sub_scale


def _sample_scores(q_ref, kn_ref, bias_ref, kt_pages, s_ref, *, n_heads, dec_seq):
    n_pages = len(kt_pages)
    att_w = q_ref.shape[2]
    n_rows = 2 * n_heads * dec_seq
    q_rep = jnp.concatenate([q_ref[0]] * (2 * n_heads), axis=0)
    row_blk = lax.broadcasted_iota(jnp.int32, (n_rows, att_w), 0) // dec_seq
    col_blk = lax.broadcasted_iota(jnp.int32, (n_rows, att_w), 1) // HEAD_DIM
    q_exp = jnp.where(row_blk == col_blk, q_rep, 0.0).astype(BF16)
    pad = jnp.zeros((PAGE_SIZE - dec_seq, att_w), F32)
    k_new = jnp.concatenate([kn_ref[0], pad], axis=0).astype(BF16)
    m_el = None
    for p in range(n_pages + 1):
        if p < n_pages:
            s = _dot(q_exp, kt_pages[p][...].astype(BF16))
            if p == n_pages - 1:
                s = s + bias_ref[0]
        else:
            s = _dot_nt(q_exp, k_new) + bias_ref[1]
        s_ref[p] = s
        m_el = s if m_el is None else jnp.maximum(m_el, s)
    return jnp.max(m_el, -1, keepdims=True)


def _sample_values(m, s_ref, vn_ref, v_pages, lam, g, *, n_heads, dec_seq, sub_scale):
    n_pages = len(v_pages)
    att_w = vn_ref.shape[2]
    head_rows = 2 * dec_seq
    pad = jnp.zeros((PAGE_SIZE - dec_seq, att_w), F32)
    v_new = jnp.concatenate([vn_ref[0], pad], axis=0).astype(BF16)
    l_el = jnp.zeros(s_ref.shape[1:], F32)
    accs = [jnp.zeros((head_rows, HEAD_WIDTH), F32) for _ in range(n_heads)]
    for p in range(n_pages + 1):
        e = jnp.exp2(s_ref[p] - m)
        l_el = l_el + e
        eb = e.astype(BF16)
        for h in range(n_heads):
            if p < n_pages:
                v_h = v_pages[p][pl.ds(h, PAGE_SIZE, stride=n_heads), :].astype(BF16)
            else:
                v_h = v_new[:, h * HEAD_WIDTH:(h + 1) * HEAD_WIDTH]
            accs[h] = accs[h] + _dot(eb[h * head_rows:(h + 1) * head_rows, :], v_h)
    l = jnp.sum(l_el, -1, keepdims=True)
    heads = []
    for h in range(n_heads):
        o = accs[h] / l[h * head_rows:(h + 1) * head_rows]
        heads.append(_sub_norm(o[:dec_seq], o[dec_seq:], lam, g, sub_scale))
    return jnp.concatenate(heads, axis=-1)


def _attn_kernel(pt_ref, lam_ref, q_ref, k_ref, v_ref, near_ref, mb_ref, mm_ref, g_ref,
                 qs_ref, kn_ref, vn_ref, bias_ref, *rest, layer, tq, n_q, n_pages, n_heads, dec_seq,
                 sub_scale):
    del pt_ref
    kt_pages = rest[:n_pages]
    v_pages = rest[n_pages:2 * n_pages]
    o_ref, kt_ref, os_ref, k1_ref, k2_ref, v1_ref, s_ref = rest[-7:]
    pair = pl.program_id(2)
    lam = lam_ref[layer]
    g = g_ref[...]
    t = k_ref.shape[1]

    def scores(rows, spans):
        q_t = q_ref[0, rows, :]
        comps = []
        for kc_ref in (k1_ref, k2_ref):
            blocks = []
            for k0, size, bias in spans:
                s = _dot_nt(q_t, kc_ref[k0:k0 + size, :])
                blocks.append((s if bias is None else s + bias, v1_ref[k0:k0 + size, :]))
            comps.append(blocks)
        return comps

    def finish(rows, comps):
        outs = [_softmax_pv(blocks) for blocks in comps]
        o_ref[0, rows, :] = _sub_norm(outs[0], outs[1], lam, g, sub_scale).astype(BF16)

    @pl.when(pair == 0)
    def _():
        k = k_ref[0]
        lane = lax.broadcasted_iota(jnp.int32, k.shape, 1)
        k1_ref[...] = jnp.where(lane < HEAD_DIM, k, 0.0).astype(BF16)
        k2_ref[...] = jnp.where(lane >= HEAD_DIM, k, 0.0).astype(BF16)
        v1_ref[:, :HEAD_WIDTH] = v_ref[0]
        v1_ref[:, HEAD_WIDTH:] = (lane == 0).astype(F32).astype(BF16)
        n_full = t // LANES
        for c in range(n_full):
            kt_ref[:, c * LANES:(c + 1) * LANES] = k_ref[0, c * LANES:(c + 1) * LANES, :].T
        tail = t - n_full * LANES
        if tail:
            k_tail = jnp.concatenate([k_ref[0, n_full * LANES:, :],
                                      jnp.zeros((LANES - tail, HEAD_WIDTH), F32)], axis=0)
            kt_ref[:, n_full * LANES:] = k_tail.T[:, :tail]
        meta_rows = slice(0, N_META)
        finish(meta_rows, scores(meta_rows, [(0, N_META, mm_ref[0])]))

    def tile(i):
        r0 = N_META + i * tq
        if i == 0:
            spans = [(0, N_META, mb_ref[0]), (r0, tq, near_ref[0, :, tq:])]
        else:
            spans = [(0, r0 - tq, None), (r0 - tq, 2 * tq, near_ref[0])]
        return slice(r0, r0 + tq), spans

    sample = dict(n_heads=n_heads, dec_seq=dec_seq)
    for pr in range(n_q // 2):
        @pl.when(pair == pr)
        def _(pr=pr):
            first, second = tile(pr), tile(n_q - 1 - pr)
            m = _sample_scores(qs_ref, kn_ref, bias_ref, kt_pages, s_ref, **sample)
            comps_first = scores(*first)
            comps_second = scores(*second)
            os_ref[0] = _sample_values(m, s_ref, vn_ref, v_pages, lam, g, sub_scale=sub_scale,
                                       **sample)
            finish(first[0], comps_first)
            finish(second[0], comps_second)


def _attn(page_table, lam, q, k, vb, near, mb, mm, g, kt_all, qs, ks_new, vs_new, bias_s, cache_kt,
          cache_v, *, layer, depth, sub_scale):
    bsz, t, att_w = q.shape
    dec_b, dec_seq, _ = qs.shape
    n_pages = page_table.shape[1]
    n_heads = att_w // HEAD_WIDTH
    tq = near.shape[1]
    n_q = (t - N_META) // tq
    n_pairs = n_q // 2
    assert n_q * tq + N_META == t and tq > MAX_DISTANCE and n_q % 2 == 0
    assert dec_b == bsz * n_heads * n_pairs
    item = lambda b, h, p: (b * n_heads + h) * n_pairs + p
    seq = pl.BlockSpec((1, t, HEAD_WIDTH), lambda b, h, p, pt: (b, 0, h))
    per_head = lambda a: pl.BlockSpec((1,) + a.shape[1:], lambda b, h, p, pt: (h, 0, 0))
    per_item = pl.BlockSpec((1, dec_seq, att_w), lambda b, h, p, pt: (item(b, h, p), 0, 0))

    def page_spec(pg):
        return pl.BlockSpec((None, None) + cache_v.shape[2:],
                            lambda b, h, p, pt: (layer, pt[item(b, h, p) * n_pages + pg], 0, 0))

    pages = [page_spec(pg) for pg in range(n_pages)]
    in_specs = [pl.BlockSpec(memory_space=pltpu.SMEM), seq, seq, seq, per_head(near), per_head(mb),
                per_head(mm), _of_layer(g, layer), per_item, per_item, per_item,
                pl.BlockSpec(bias_s.shape, lambda b, h, p, pt: (0, 0, 0))] + pages + pages
    args = (lam, q, k, vb, near, mb, mm, g, qs, ks_new, vs_new, bias_s,
            *([cache_kt] * n_pages), *([cache_v] * n_pages))
    aliases = {}
    if kt_all is not None:
        in_specs.append(pl.BlockSpec(memory_space=pl.ANY))
        args += (kt_all,)
        aliases = {len(args): 1}
    grid_spec = pltpu.PrefetchScalarGridSpec(
        num_scalar_prefetch=1,
        grid=(bsz, n_heads, n_pairs),
        in_specs=in_specs,
        out_specs=[seq, pl.BlockSpec((None, None, HEAD_WIDTH, t),
                                     lambda b, h, p, pt: (layer, b, h, 0)), per_item],
        scratch_shapes=[pltpu.VMEM((t, HEAD_WIDTH), BF16), pltpu.VMEM((t, HEAD_WIDTH), BF16),
                        pltpu.VMEM((t, 2 * HEAD_WIDTH), BF16),
                        pltpu.VMEM((n_pages + 1, 2 * n_heads * dec_seq, PAGE_SIZE), F32)],
    )
    return pl.pallas_call(
        functools.partial(_attn_kernel, layer=layer, tq=tq, n_q=n_q, n_pages=n_pages,
                          n_heads=n_heads, dec_seq=dec_seq, sub_scale=sub_scale),
        grid_spec=grid_spec,
        out_shape=[jax.ShapeDtypeStruct(q.shape, BF16),
                   jax.ShapeDtypeStruct((depth, bsz, att_w, t), F32),
                   jax.ShapeDtypeStruct(qs.shape, F32)],
        input_output_aliases=aliases,
        compiler_params=pltpu.CompilerParams(
            dimension_semantics=("parallel", "parallel", "arbitrary"),
            vmem_limit_bytes=V7X_VMEM_LIMIT),
        name="attn",
    )(page_table.reshape(-1), *args)


def _ssm_kernel(u_ref, h0r_ref, h0i_ref, lbr_ref, lbi_ref, br_ref, bi_ref, cr_ref, ci_ref, d_ref,
                z_ref, hr_ref, hi_ref, u_tm, z_tm, *x_scratch, nb, tc):
    n_parts = u_tm.shape[0]
    part_s = lbr_ref.shape[1] // n_parts
    n_rows = nb * tc
    x_re, x_im = x_scratch[:n_parts], x_scratch[n_parts:]

    @pl.when(pl.program_id(0) == 0)
    def _():
        hr_ref[...] = h0r_ref[...]
        hi_ref[...] = h0i_ref[...]

    def to_time_major(b, c):
        ub = u_ref[b]
        for p in range(n_parts):
            u_tm[p, pl.ds(b, tc, stride=nb), :] = ub[:, p * LANES:(p + 1) * LANES]
        return c

    lax.fori_loop(0, nb, to_time_major, 0, unroll=(nb == 8))

    def project_in(p):
        up = u_tm[p].astype(BF16)
        x_re[p][...] = _dot(up, br_ref[p])
        x_im[p][...] = _dot(up, bi_ref[p])

    def recur(p):
        gcols = slice(p * part_s, (p + 1) * part_s)
        lr = lbr_ref[:, gcols]
        li = lbi_ref[:, gcols]

        def rows(r0):
            hr = hr_ref[pl.ds(r0, 8), gcols]
            hi = hi_ref[pl.ds(r0, 8), gcols]
            for t in range(tc):
                row = pl.ds(t * nb + r0, 8)
                hr, hi = (lr * hr - li * hi + x_re[p][row, :], lr * hi + li * hr + x_im[p][row, :])
                x_re[p][row, :] = hr
                x_im[p][row, :] = hi
            hr_ref[pl.ds(r0, 8), gcols] = hr
            hi_ref[pl.ds(r0, 8), gcols] = hi

        if nb == 8:
            rows(0)
        else:
            lax.fori_loop(0, nb // 8, lambda rg, c: (rows(pl.multiple_of(rg * 8, 8)), c)[1], 0)

    def project_out(p):
        rc = n_rows if n_rows <= 512 else 256
        for r0 in range(0, n_rows, rc):
            rows = slice(r0, r0 + rc)
            y = (_dot(x_re[p][rows, :].astype(BF16), cr_ref[p])
                 + _dot(x_im[p][rows, :].astype(BF16), ci_ref[p])
                 + d_ref[:, p * LANES:(p + 1) * LANES] * u_tm[p, rows, :])
            z_tm[p, rows, :] = jax.nn.gelu(y)

    for step in range(n_parts + 2):
        if step < n_parts:
            project_in(step)
        if 1 <= step <= n_parts:
            recur(step - 1)
        if step >= 2:
            project_out(step - 2)

    def to_batch_major(b, c):
        z_ref[b] = jnp.concatenate(
            [z_tm[p, pl.ds(b, tc, stride=nb), :] for p in range(n_parts)], axis=-1)
        return c

    lax.fori_loop(0, nb, to_batch_major, 0, unroll=(nb == 8))


def _ssm(u, h0_re, h0_im, tables, *, layer, tc):
    nb, t, ssm_w = u.shape
    n_state = tables[0].shape[-1]
    n_parts = ssm_w // LANES
    assert t % tc == 0 and nb % 8 == 0 and tables[2].shape[1:] == (n_parts, LANES, n_state // n_parts)
    state = pl.BlockSpec((nb, n_state), lambda i: (0, 0))
    seq = pl.BlockSpec((nb, tc, ssm_w), lambda i: (0, i, 0))
    return pl.pallas_call(
        functools.partial(_ssm_kernel, nb=nb, tc=tc),
        grid=(t // tc,),
        in_specs=[seq, state, state] + [_of_layer(a, layer) for a in tables],
        out_specs=[seq, state, state],
        out_shape=[jax.ShapeDtypeStruct(u.shape, F32), jax.ShapeDtypeStruct((nb, n_state), F32),
                   jax.ShapeDtypeStruct((nb, n_state), F32)],
        scratch_shapes=[pltpu.VMEM((n_parts, nb * tc, LANES), F32)] * 2
        + [pltpu.VMEM((nb * tc, n_state // n_parts), F32)] * (2 * n_parts),
        compiler_params=pltpu.CompilerParams(dimension_semantics=("arbitrary",),
                                             vmem_limit_bytes=V7X_VMEM_LIMIT),
        name="ssm",
    )(u, h0_re, h0_im, *tables)


def _merge_kernel(x_ref, on_ref, z_ref, wgate_ref, wau_ref, wglu_ref, wout_ref, g_ref, b_ref,
                  o_ref, *, alpha):
    d = x_ref.shape[1]
    x = x_ref[...]
    xb = x.astype(BF16)
    on = on_ref[...].astype(BF16)
    zb = z_ref[...].astype(BF16)

    def branch_matmuls(c0):
        cols = slice(c0, c0 + MERGE_CHUNK)
        gcols = slice(d + c0, d + c0 + MERGE_CHUNK)
        return (_dot(xb, wgate_ref[:, cols]), _dot(xb, wgate_ref[:, gcols]),
                _dot(on, wau_ref[:, cols]), _dot(zb, wglu_ref[:, cols]), _dot(zb, wglu_ref[:, gcols]))

    def mix_out(c0, parts):
        ga, gb, a_branch, glu_v, glu_g = parts
        b_branch = glu_v * jax.nn.sigmoid(glu_g)
        mix_in = jax.nn.sigmoid(ga) * a_branch + jax.nn.sigmoid(gb) * b_branch
        return _dot(mix_in.astype(BF16), wout_ref[c0:c0 + MERGE_CHUNK, :])

    acc = alpha * x
    pending = None
    for c0 in range(0, d, MERGE_CHUNK):
        parts = branch_matmuls(c0)
        if pending is not None:
            acc = acc + mix_out(*pending)
        pending = (c0, parts)
    acc = acc + mix_out(*pending)
    o_ref[...] = _layer_norm(acc, g_ref[...], b_ref[...])


def _merge(x, o_n, z, w_gate, w_att_up, w_glu, w_out, ln_g, ln_b, *, layer, tm, alpha):
    n, d = x.shape
    row = lambda a: pl.BlockSpec((tm, a.shape[1]), lambda i: (i, 0))
    params = (w_gate, w_att_up, w_glu, w_out, ln_g, ln_b)
    return pl.pallas_call(
        functools.partial(_merge_kernel, alpha=alpha),
        grid=(n // tm,),
        in_specs=[row(x), row(o_n), row(z)] + [_of_layer(a, layer) for a in params],
        out_specs=row(x),
        out_shape=jax.ShapeDtypeStruct(x.shape, F32),
        compiler_params=pltpu.CompilerParams(dimension_semantics=("parallel",),
                                             vmem_limit_bytes=V7X_VMEM_LIMIT),
        name="merge",
    )(x, o_n, z, w_gate, w_att_up, w_glu, w_out, ln_g, ln_b)


def _ffn_kernel(x_ref, wup_ref, wdn_ref, g_ref, b_ref, o_ref, *, alpha, ff_chunk):
    x = x_ref[...]
    xb = x.astype(BF16)
    acc = alpha * x
    for c0 in range(0, wup_ref.shape[1], ff_chunk):
        hid = jnp.maximum(_dot(xb, wup_ref[:, c0:c0 + ff_chunk]), 0.0)
        acc = acc + _dot((hid * hid).astype(BF16), wdn_ref[c0:c0 + ff_chunk, :])
    o_ref[...] = _layer_norm(acc, g_ref[...], b_ref[...])


def _ffn(x, w_up, w_down, ln_g, ln_b, *, layer, tm, alpha):
    n, d = x.shape
    row = pl.BlockSpec((tm, d), lambda i: (i, 0))
    return pl.pallas_call(
        functools.partial(_ffn_kernel, alpha=alpha, ff_chunk=d),
        grid=(n // tm,),
        in_specs=[row] + [_of_layer(a, layer) for a in (w_up, w_down, ln_g, ln_b)],
        out_specs=row,
        out_shape=jax.ShapeDtypeStruct(x.shape, F32),
        compiler_params=pltpu.CompilerParams(dimension_semantics=("parallel",),
                                             vmem_limit_bytes=V7X_VMEM_LIMIT),
        name="ffn",
    )(x, w_up, w_down, ln_g, ln_b)


def _ffn_skip_meta(x, w_up, w_down, ln_g, ln_b, *, layer, tm, alpha):
    bsz, t, d = x.shape
    seq = t - N_META
    assert seq % tm == 0
    window = pl.BlockSpec((pl.Squeezed(), pl.Element(tm), pl.Element(d)),
                          lambda b, j: (b, pl.multiple_of(N_META + j * tm, N_META), 0))
    return pl.pallas_call(
        functools.partial(_ffn_kernel, alpha=alpha, ff_chunk=d),
        grid=(bsz, seq // tm),
        in_specs=[window] + [_of_layer(a, layer) for a in (w_up, w_down, ln_g, ln_b)],
        out_specs=pl.BlockSpec((None, tm, d), lambda b, j: (b, j, 0)),
        out_shape=jax.ShapeDtypeStruct((bsz, seq, d), F32),
        compiler_params=pltpu.CompilerParams(dimension_semantics=("parallel", "parallel"),
                                             vmem_limit_bytes=V7X_VMEM_LIMIT),
        name="ffn_skip_meta",
    )(x, w_up, w_down, ln_g, ln_b)


def _ssm_tables(a_re, a_im, log_dt, b_re, b_im, c_re, c_im, d):
    depth, n_groups, n_p = a_re.shape
    a_re = a_re.astype(F32)
    a_im = a_im.astype(F32)
    dt = jnp.exp(log_dt.astype(F32))[..., None]
    mag = jnp.exp(a_re * dt)
    lb_re = mag * jnp.cos(a_im * dt)
    lb_im = mag * jnp.sin(a_im * dt)
    nr, ni = lb_re - 1.0, lb_im
    den = a_re * a_re + a_im * a_im
    f_re = (nr * a_re + ni * a_im) / den
    f_im = (ni * a_re - nr * a_im) / den
    b_re = b_re.astype(F32)
    b_im = b_im.astype(F32)
    bb_re = f_re[..., None] * b_re - f_im[..., None] * b_im
    bb_im = f_re[..., None] * b_im + f_im[..., None] * b_re
    gpp = LANES // GROUP_CH
    n_parts = n_groups // gpp
    eye = jnp.eye(gpp, dtype=F32)

    def in_mat(bb):
        bb = bb.reshape(depth, n_parts, gpp, n_p, GROUP_CH)
        full = jnp.einsum('lngpc,gh->lngchp', bb, eye)
        return full.reshape(depth, n_parts, LANES, gpp * n_p).astype(BF16)

    def out_mat(cc):
        cc = cc.reshape(depth, n_parts, gpp, GROUP_CH, n_p)
        full = jnp.einsum('lngcp,gh->lngphc', cc, eye)
        return full.reshape(depth, n_parts, gpp * n_p, LANES).astype(BF16)

    sublanes = lambda a: jnp.broadcast_to(a.reshape(depth, 1, -1), (depth, 8, n_groups * n_p))
    return (sublanes(lb_re), sublanes(lb_im), in_mat(bb_re), in_mat(bb_im),
            out_mat(c_re.astype(F32)), out_mat(-c_im.astype(F32)), d.astype(F32).reshape(depth, 1, -1))


def kernel(x_prompt, x_sample, cache_k, cache_v, state_ssm_re, state_ssm_im, page_table, meta_tokens, ln_in_g, ln_in_b, rel_bias, w_in, lambda_q1, lambda_k1, lambda_q2, lambda_k2, subln_g, w_att_up, ssm_a_re, ssm_a_im, ssm_log_dt, ssm_b_re, ssm_b_im, ssm_c_re, ssm_c_im, ssm_d, w_glu, w_out, ln1_g, ln1_b, w_up, w_down, ln2_g, ln2_b):
    bsz, seq, d_model = x_prompt.shape
    dec_b, dec_seq, _ = x_sample.shape
    depth = w_in.shape[0]
    att_w = w_att_up.shape[1]
    ssm_w = w_glu.shape[1]
    n_heads = att_w // HEAD_WIDTH
    n_groups = ssm_w // GROUP_CH
    n_state = n_groups * STATE_DIM
    t = N_META + seq
    past = page_table.shape[1] * PAGE_SIZE
    alpha = (2.0 * depth) ** 0.25
    assert seq % ATT_Q_TILE == 0 and t % SSM_CHUNK == 0 and past >= 2 * PAGE_SIZE

    tm_p = _row_tile(t, 704)
    tm_s = _row_tile(dec_b * dec_seq, 512)

    xp = x_prompt
    kt_all = v_all = None
    xs = x_sample.reshape(dec_b * dec_seq, d_model)
    row = lambda a: a.astype(F32).reshape(1, -1)
    near, mb, mm, bias_s = _bias_tables(rel_bias, ATT_Q_TILE, dec_seq)
    n_pool = cache_k.shape[1]
    cache_kt = jnp.transpose(cache_k, (0, 1, 3, 4, 5, 2)).reshape(depth, n_pool, att_w, PAGE_SIZE)
    cache_v = cache_v.reshape(depth, n_pool, PAGE_SIZE * n_heads, HEAD_WIDTH)
    h0p = jnp.zeros((bsz, n_state), F32)

    lam_init = [0.8 - 0.6 * math.exp(-0.3 * l) for l in range(depth)]
    f32 = lambda a: a.astype(F32)
    lam = (jnp.exp(jnp.sum(f32(lambda_q1) * f32(lambda_k1), -1))
           - jnp.exp(jnp.sum(f32(lambda_q2) * f32(lambda_k2), -1)) + jnp.asarray(lam_init, F32))
    rows = lambda a: a.astype(F32).reshape(depth, 1, -1)
    g_sub = rows(subln_g)
    n_qkvu = 3 * att_w + ssm_w
    w_qkvu_bf = w_in[:, :, :n_qkvu].astype(BF16)
    w_gate_bf = w_in[:, :, n_qkvu:].astype(BF16)
    merge_params = (w_gate_bf, w_att_up.astype(BF16), w_glu.astype(BF16), w_out.astype(BF16),
                    rows(ln1_g), rows(ln1_b))
    ffn_params = (w_up.astype(BF16), w_down.astype(BF16), rows(ln2_g), rows(ln2_b))
    ssm_tables = _ssm_tables(ssm_a_re, ssm_a_im, ssm_log_dt, ssm_b_re, ssm_b_im, ssm_c_re, ssm_c_im,
                             ssm_d)
    state_re = state_ssm_re.reshape(depth, dec_b, n_state).astype(F32)
    state_im = state_ssm_im.reshape(depth, dec_b, n_state).astype(F32)

    outs = {name: [] for name in ("hpr", "hpi", "ks", "vs", "hsr", "hsi")}
    for l in range(depth):
        sub_scale = 1.0 - lam_init[l]

        *x_ln, q, kf, vb, u, v_all = _in_proj_prompt(
            xp, meta_tokens.astype(F32), row(ln_in_g), row(ln_in_b), w_qkvu_bf, v_all, layer=l,
            depth=depth, bsz=bsz, t=t, tm=tm_p, att_w=att_w, ssm_w=ssm_w)
        xp = x_ln[0] if x_ln else xp
        *x_ln, qs, ks_new, vs_new, us = _in_proj(xs, row(ln_in_g), row(ln_in_b), w_qkvu_bf, layer=l,
                                                 tm=tm_s, att_w=att_w, ssm_w=ssm_w)
        xs = x_ln[0] if x_ln else xs
        seq3 = lambda a: a.reshape(bsz, t, a.shape[-1])
        dec3 = lambda a: a.reshape(dec_b, dec_seq, a.shape[-1])
        o_n, kt_all, os_n = _attn(page_table, lam, seq3(q), seq3(kf), seq3(vb), near, mb, mm, g_sub,
                                  kt_all, dec3(qs), dec3(ks_new), dec3(vs_new), bias_s,
                                  cache_kt, cache_v, layer=l, depth=depth, sub_scale=sub_scale)

        z, hpr, hpi = _ssm(seq3(u), h0p, h0p, ssm_tables, layer=l, tc=SSM_CHUNK)
        xp = _merge(xp, o_n.reshape(bsz * t, att_w), z.reshape(bsz * t, ssm_w), *merge_params,
                    layer=l, tm=tm_p, alpha=alpha)
        if l < depth - 1:
            xp = _ffn(xp, *ffn_params, layer=l, tm=tm_p, alpha=alpha)
        else:
            y_prompt = _ffn_skip_meta(xp.reshape(bsz, t, d_model), *ffn_params, layer=l,
                                      tm=_row_tile(seq, 512), alpha=alpha)
        outs["hpr"].append(hpr.reshape(bsz, n_groups, STATE_DIM))
        outs["hpi"].append(hpi.reshape(bsz, n_groups, STATE_DIM))

        z, hsr, hsi = _ssm(dec3(us), state_re[l], state_im[l], ssm_tables, layer=l, tc=dec_seq)
        xs = _merge(xs, os_n.reshape(dec_b * dec_seq, att_w), z.reshape(dec_b * dec_seq, ssm_w),
                    *merge_params, layer=l, tm=tm_s, alpha=alpha)
        xs = _ffn(xs, *ffn_params, layer=l, tm=tm_s, alpha=alpha)
        outs["ks"].append(ks_new.reshape(dec_b, dec_seq, n_heads, 2, HEAD_DIM))
        outs["vs"].append(vs_new.reshape(dec_b, dec_seq, n_heads, HEAD_WIDTH))
        outs["hsr"].append(hsr.reshape(dec_b, n_groups, STATE_DIM))
        outs["hsi"].append(hsi.reshape(dec_b, n_groups, STATE_DIM))

    y_sample = xs.reshape(dec_b, dec_seq, d_model)
    st = lambda name: jnp.stack(outs[name])
    k_prompt = jnp.transpose(kt_all.reshape(depth, bsz, n_heads, 2, HEAD_DIM, t), (0, 1, 5, 2, 3, 4))
    v_prompt = v_all.reshape(depth, bsz, t, n_heads, HEAD_WIDTH)
    return (y_prompt, y_sample, k_prompt, v_prompt, st("hpr"), st("hpi"), st("ks"), st("vs"),
            st("hsr"), st("hsi"))
```

```python
import functools
import math

import jax
import jax.numpy as jnp
from jax import lax
from jax.experimental import pallas as pl
from jax.experimental.pallas import tpu as pltpu

F32 = jnp.float32
BF16 = jnp.bfloat16

N_META = 16
HEAD_DIM = 64
HEAD_WIDTH = 2 * HEAD_DIM
GROUP_CH = 16
STATE_DIM = 64
PAGE_SIZE = 128
LANES = 128
N_BUCKETS = 32
MAX_DISTANCE = 128
EPS = 1e-5
LOG2E = math.log2(math.e)

ATT_Q_TILE = 256
MERGE_CHUNK = 256
SSM_CHUNK = 48
V7X_VMEM_BYTES = 64 * 1024 * 1024
V7X_VMEM_LIMIT = V7X_VMEM_BYTES * 7 // 8

assert (N_BUCKETS // 2 + int(math.log((MAX_DISTANCE + 1) / (N_BUCKETS // 2))
                             / math.log(MAX_DISTANCE / (N_BUCKETS // 2)) * (N_BUCKETS // 2))
        >= N_BUCKETS - 1)


def _layer_norm(x, g, b):
    mu = jnp.mean(x, -1, keepdims=True)
    xc = x - mu
    var = jnp.mean(xc * xc, -1, keepdims=True)
    return xc * lax.rsqrt(var + EPS) * g + b


def _dot(a, b):
    return jnp.dot(a, b, preferred_element_type=F32)


def _dot_nt(a, b):
    return lax.dot_general(a, b, (((1,), (1,)), ((), ())), preferred_element_type=F32)


def _resident(shape):
    return pl.BlockSpec(shape, lambda *_: (0,) * len(shape), pipeline_mode=pl.Buffered(1))


def _of_layer(a, layer):
    return pl.BlockSpec((None,) + a.shape[1:], lambda *_: (layer,) + (0,) * (a.ndim - 1),
                        pipeline_mode=pl.Buffered(1))


def _row_tile(n_rows, target):
    best = None
    for t in range(16, min(n_rows, target) + 1, 16):
        if n_rows % t == 0:
            best = t
    assert best is not None, n_rows
    return best


def _project_qkvu(x, w_ref, att_w, ssm_w):
    xb = x.astype(BF16)
    widths = (att_w, att_w, att_w, ssm_w)
    starts = [sum(widths[:i]) for i in range(len(widths))]
    q, k, v, u = (_dot(xb, w_ref[:, c0:c0 + w]) for c0, w in zip(starts, widths))
    return (q * (HEAD_DIM ** -0.5 * LOG2E)).astype(BF16), k, v, u


def _in_proj_kernel(x_ref, g_ref, b_ref, w_ref, *out_refs, apply_ln, att_w, ssm_w):
    x = x_ref[...]
    if apply_ln:
        xo_ref, *out_refs = out_refs
        x = _layer_norm(x, g_ref[...], b_ref[...])
        xo_ref[...] = x
    q_ref, k_ref, v_ref, u_ref = out_refs
    q, k_ref[...], v_ref[...], u_ref[...] = _project_qkvu(x, w_ref, att_w, ssm_w)
    q_ref[...] = q.astype(F32)


def _in_proj(x, ln_g, ln_b, w_qkvu, *, layer, tm, att_w, ssm_w):
    n, d = x.shape
    apply_ln = layer == 0
    row = lambda w: pl.BlockSpec((tm, w), lambda i: (i, 0))
    outs = [(att_w, F32), (att_w, F32), (att_w, F32), (ssm_w, F32)]
    if apply_ln:
        outs = [(d, F32)] + outs
    return pl.pallas_call(
        functools.partial(_in_proj_kernel, apply_ln=apply_ln, att_w=att_w, ssm_w=ssm_w),
        grid=(n // tm,),
        in_specs=[row(d), _resident((1, d)), _resident((1, d)), _of_layer(w_qkvu, layer)],
        out_specs=[row(w) for w, _ in outs],
        out_shape=[jax.ShapeDtypeStruct((n, w), dt) for w, dt in outs],
        compiler_params=pltpu.CompilerParams(dimension_semantics=("parallel",),
                                             vmem_limit_bytes=V7X_VMEM_LIMIT),
        name="in_proj",
    )(x, ln_g, ln_b, w_qkvu)


def _in_proj_prompt_kernel(*refs, first, tm, att_w, ssm_w):
    if first:
        x_ref, meta_ref, g_ref, b_ref, w_ref, xo_ref, q_ref, k_ref, vb_ref, u_ref, vall_ref = refs
    else:
        x_ref, w_ref, _, q_ref, k_ref, vb_ref, u_ref, vall_ref = refs
    n_heads = att_w // HEAD_WIDTH

    def body(x):
        if first:
            x = _layer_norm(x, g_ref[...], b_ref[...])
            xo_ref[...] = x
        q_ref[...], k_ref[...], v, u_ref[...] = _project_qkvu(x, w_ref, att_w, ssm_w)
        vb_ref[...] = v.astype(BF16)
        for h in range(n_heads):
            vall_ref[pl.ds(h, tm, stride=n_heads), :] = v[:, h * HEAD_WIDTH:(h + 1) * HEAD_WIDTH]

    if first:
        @pl.when(pl.program_id(1) == 0)
        def _():
            body(jnp.concatenate([meta_ref[...], x_ref[:tm - N_META, :]], axis=0))

        @pl.when(pl.program_id(1) != 0)
        def _():
            body(x_ref[...])
    else:
        body(x_ref[...])


def _in_proj_prompt(x, meta, ln_g, ln_b, w_qkvu, v_all, *, layer, depth, bsz, t, tm, att_w, ssm_w):
    first = layer == 0
    d = x.shape[-1]
    n = bsz * t
    n_j = t // tm
    n_heads = att_w // HEAD_WIDTH
    row = lambda w: pl.BlockSpec((tm, w), lambda b, j: (b * n_j + j, 0))
    outs = [(att_w, BF16), (att_w, F32), (att_w, BF16), (ssm_w, F32)]
    if first:
        window = pl.BlockSpec(
            (pl.Squeezed(), pl.Element(tm), pl.Element(d)),
            lambda b, j: (b, pl.multiple_of(jnp.maximum(j * tm - N_META, 0), N_META), 0))
        in_specs = [window, _resident(meta.shape), _resident((1, d)), _resident((1, d)),
                    _of_layer(w_qkvu, layer)]
        args = (x, meta, ln_g, ln_b, w_qkvu)
        outs = [(d, F32)] + outs
        aliases = {}
    else:
        in_specs = [row(d), _of_layer(w_qkvu, layer), pl.BlockSpec(memory_space=pl.ANY)]
        args = (x, w_qkvu, v_all)
        aliases = {2: len(outs)}
    vall_shape = (depth, n * n_heads, HEAD_WIDTH)
    return pl.pallas_call(
        functools.partial(_in_proj_prompt_kernel, first=first, tm=tm, att_w=att_w, ssm_w=ssm_w),
        grid=(bsz, n_j),
        in_specs=in_specs,
        out_specs=[row(w) for w, _ in outs]
        + [pl.BlockSpec((None, tm * n_heads, HEAD_WIDTH), lambda b, j: (layer, b * n_j + j, 0))],
        out_shape=[jax.ShapeDtypeStruct((n, w), dt) for w, dt in outs]
        + [jax.ShapeDtypeStruct(vall_shape, F32)],
        input_output_aliases=aliases,
        compiler_params=pltpu.CompilerParams(dimension_semantics=("parallel", "parallel"),
                                             vmem_limit_bytes=V7X_VMEM_LIMIT),
        name="in_proj_prompt",
    )(*args)


def _bias_kernel(rb_ref, rel_ref, o_ref):
    rel = rel_ref[...]
    n = jnp.maximum(rel, 0)
    exact = N_BUCKETS // 2
    large = exact + (jnp.log(jnp.maximum(n, 1).astype(F32) / exact)
                     / math.log(MAX_DISTANCE / exact) * (N_BUCKETS - exact)).astype(jnp.int32)
    bucket = jnp.where(n < exact, n, jnp.minimum(large, N_BUCKETS - 1))
    for h in range(o_ref.shape[0]):
        far = rb_ref[N_BUCKETS - 1, h]
        acc = jnp.zeros(rel.shape, F32)
        for b in range(N_BUCKETS - 1):
            acc = jnp.where(bucket == b, (rb_ref[b, h] - far) * LOG2E, acc)
        o_ref[h] = jnp.where(rel >= 0, acc, -jnp.inf)


def _bias_tile(rel_bias, rel):
    n_heads = rel_bias.shape[1]
    return pl.pallas_call(
        _bias_kernel,
        in_specs=[pl.BlockSpec(memory_space=pltpu.SMEM), pl.BlockSpec(rel.shape, lambda: (0, 0))],
        out_specs=pl.BlockSpec((n_heads,) + rel.shape, lambda: (0, 0, 0)),
        out_shape=jax.ShapeDtypeStruct((n_heads,) + rel.shape, F32),
        name="bias_tile",
    )(rel_bias.astype(F32), rel)


def _bias_tables(rel_bias, tq, dec_seq):
    i32 = jnp.int32
    n_heads = rel_bias.shape[1]
    r = jnp.arange(tq, dtype=i32)[:, None]
    near = _bias_tile(rel_bias, tq + r - jnp.arange(2 * tq, dtype=i32)[None, :])
    mb = _bias_tile(rel_bias, N_META + r - jnp.arange(N_META, dtype=i32)[None, :])
    mr = jnp.arange(N_META, dtype=i32)
    mm = _bias_tile(rel_bias, mr[:, None] - mr[None, :])
    s = jnp.arange(dec_seq, dtype=i32)[:, None]
    kk = jnp.arange(PAGE_SIZE, dtype=i32)[None, :]
    last = _bias_tile(rel_bias, PAGE_SIZE + s - kk)
    new = _bias_tile(rel_bias, jnp.where(kk < dec_seq, s - kk, -1))

    def rows(tile):
        return jnp.broadcast_to(tile[:, None], (n_heads, 2, dec_seq, PAGE_SIZE)).reshape(-1, PAGE_SIZE)

    return near, mb, mm, jnp.stack([rows(last), rows(new)])


def _softmax_pv(blocks):
    m = functools.reduce(jnp.maximum, [jnp.max(s, -1, keepdims=True) for s, _ in blocks])
    acc = 0.0
    for s, v_ones in blocks:
        acc = acc + _dot(jnp.exp2(s - m).astype(BF16), v_ones)
    return acc[:, :HEAD_WIDTH] / acc[:, HEAD_WIDTH:HEAD_WIDTH + 1]


def _sub_norm(o1, o2, lam, g, sub_scale):
    o = o1 - lam * o2
    return o * lax.rsqrt(jnp.mean(o * o, -1, keepdims=True) + EPS) * g * sub_scale


def _sample_scores(q_ref, kn_ref, bias_ref, kt_pages, s_ref, *, n_heads, dec_seq):
    n_pages = len(kt_pages)
    att_w = q_ref.shape[2]
    n_rows = 2 * n_heads * dec_seq
    q_rep = jnp.concatenate([q_ref[0]] * (2 * n_heads), axis=0)
    row_blk = lax.broadcasted_iota(jnp.int32, (n_rows, att_w), 0) // dec_seq
    col_blk = lax.broadcasted_iota(jnp.int32, (n_rows, att_w), 1) // HEAD_DIM
    q_exp = jnp.where(row_blk == col_blk, q_rep, 0.0).astype(BF16)
    pad = jnp.zeros((PAGE_SIZE - dec_seq, att_w), F32)
    k_new = jnp.concatenate([kn_ref[0], pad], axis=0).astype(BF16)
    m_el = None
    for p in range(n_pages + 1):
        if p < n_pages:
            s = _dot(q_exp, kt_pages[p][...].astype(BF16))
            if p == n_pages - 1:
                s = s + bias_ref[0]
        else:
            s = _dot_nt(q_exp, k_new) + bias_ref[1]
        s_ref[p] = s
        m_el = s if m_el is None else jnp.maximum(m_el, s)
    return jnp.max(m_el, -1, keepdims=True)


def _sample_values(m, s_ref, vn_ref, v_pages, lam, g, *, n_heads, dec_seq, sub_scale):
    n_pages = len(v_pages)
    att_w = vn_ref.shape[2]
    head_rows = 2 * dec_seq
    pad = jnp.zeros((PAGE_SIZE - dec_seq, att_w), F32)
    v_new = jnp.concatenate([vn_ref[0], pad], axis=0).astype(BF16)
    l_el = jnp.zeros(s_ref.shape[1:], F32)
    accs = [jnp.zeros((head_rows, HEAD_WIDTH), F32) for _ in range(n_heads)]
    for p in range(n_pages + 1):
        e = jnp.exp2(s_ref[p] - m)
        l_el = l_el + e
        eb = e.astype(BF16)
        for h in range(n_heads):
            if p < n_pages:
                v_h = v_pages[p][pl.ds(h, PAGE_SIZE, stride=n_heads), :].astype(BF16)
            else:
                v_h = v_new[:, h * HEAD_WIDTH:(h + 1) * HEAD_WIDTH]
            accs[h] = accs[h] + _dot(eb[h * head_rows:(h + 1) * head_rows, :], v_h)
    l = jnp.sum(l_el, -1, keepdims=True)
    heads = []
    for h in range(n_heads):
        o = accs[h] / l[h * head_rows:(h + 1) * head_rows]
        heads.append(_sub_norm(o[:dec_seq], o[dec_seq:], lam, g, sub_scale))
    return jnp.concatenate(heads, axis=-1)


def _attn_kernel(pt_ref, lam_ref, q_ref, k_ref, v_ref, near_ref, mb_ref, mm_ref, g_ref,
                 qs_ref, kn_ref, vn_ref, bias_ref, *rest, layer, tq, n_q, n_pages, n_heads, dec_seq,
                 sub_scale):
    ckt_ref, cv_ref = rest[:2]
    (o_ref, kt_ref, os_ref, k1_ref, k2_ref, v1_ref, s_ref, kbuf, vbuf, sem) = rest[-10:]
    pair = pl.program_id(2)
    n_pairs = pl.num_programs(2)
    item = (pl.program_id(0) * n_heads + pl.program_id(1)) * n_pairs + pair
    n_items = pl.num_programs(0) * n_heads * n_pairs
    slot = item % 2

    def page_copies(it, sl):
        copies = []
        for pg in range(n_pages):
            idx = pt_ref[it * n_pages + pg]
            copies.append(pltpu.make_async_copy(ckt_ref.at[layer, idx], kbuf.at[sl, pg], sem.at[0, sl]))
            copies.append(pltpu.make_async_copy(cv_ref.at[layer, idx], vbuf.at[sl, pg], sem.at[1, sl]))
        return copies

    @pl.when(item == 0)
    def _():
        for c in page_copies(0, 0):
            c.start()

    @pl.when(item + 1 < n_items)
    def _():
        for c in page_copies(item + 1, 1 - slot):
            c.start()

    for c in page_copies(item, slot):
        c.wait()
    kt_pages = [kbuf.at[slot, pg] for pg in range(n_pages)]
    v_pages = [vbuf.at[slot, pg] for pg in range(n_pages)]
    lam = lam_ref[layer]
    g = g_ref[...]
    t = k_ref.shape[1]

    def scores(rows, spans):
        q_t = q_ref[0, rows, :]
        comps = []
        for kc_ref in (k1_ref, k2_ref):
            blocks = []
            for k0, size, bias in spans:
                s = _dot_nt(q_t, kc_ref[k0:k0 + size, :])
                blocks.append((s if bias is None else s + bias, v1_ref[k0:k0 + size, :]))
            comps.append(blocks)
        return comps

    def finish(rows, comps):
        outs = [_softmax_pv(blocks) for blocks in comps]
        o_ref[0, rows, :] = _sub_norm(outs[0], outs[1], lam, g, sub_scale).astype(BF16)

    @pl.when(pair == 0)
    def _():
        k = k_ref[0]
        lane = lax.broadcasted_iota(jnp.int32, k.shape, 1)
        k1_ref[...] = jnp.where(lane < HEAD_DIM, k, 0.0).astype(BF16)
        k2_ref[...] = jnp.where(lane >= HEAD_DIM, k, 0.0).astype(BF16)
        v1_ref[:, :HEAD_WIDTH] = v_ref[0]
        v1_ref[:, HEAD_WIDTH:] = (lane == 0).astype(F32).astype(BF16)
        n_full = t // LANES
        for c in range(n_full):
            kt_ref[:, c * LANES:(c + 1) * LANES] = k_ref[0, c * LANES:(c + 1) * LANES, :].T
        tail = t - n_full * LANES
        if tail:
            k_tail = jnp.concatenate([k_ref[0, n_full * LANES:, :],
                                      jnp.zeros((LANES - tail, HEAD_WIDTH), F32)], axis=0)
            kt_ref[:, n_full * LANES:] = k_tail.T[:, :tail]
        meta_rows = slice(0, N_META)
        finish(meta_rows, scores(meta_rows, [(0, N_META, mm_ref[0])]))

    def tile(i):
        r0 = N_META + i * tq
        if i == 0:
            spans = [(0, N_META, mb_ref[0]), (r0, tq, near_ref[0, :, tq:])]
        else:
            spans = [(0, r0 - tq, None), (r0 - tq, 2 * tq, near_ref[0])]
        return slice(r0, r0 + tq), spans

    sample = dict(n_heads=n_heads, dec_seq=dec_seq)
    for pr in range(n_q // 2):
        @pl.when(pair == pr)
        def _(pr=pr):
            first, second = tile(pr), tile(n_q - 1 - pr)
            m = _sample_scores(qs_ref, kn_ref, bias_ref, kt_pages, s_ref, **sample)
            comps_first = scores(*first)
            comps_second = scores(*second)
            os_ref[0] = _sample_values(m, s_ref, vn_ref, v_pages, lam, g, sub_scale=sub_scale,
                                       **sample)
            finish(first[0], comps_first)
            finish(second[0], comps_second)


def _attn(page_table, lam, q, k, vb, near, mb, mm, g, kt_all, qs, ks_new, vs_new, bias_s, cache_kt,
          cache_v, *, layer, depth, sub_scale):
    bsz, t, att_w = q.shape
    dec_b, dec_seq, _ = qs.shape
    n_pages = page_table.shape[1]
    n_heads = att_w // HEAD_WIDTH
    tq = near.shape[1]
    n_q = (t - N_META) // tq
    n_pairs = n_q // 2
    assert n_q * tq + N_META == t and tq > MAX_DISTANCE and n_q % 2 == 0
    assert dec_b == bsz * n_heads * n_pairs
    item = lambda b, h, p: (b * n_heads + h) * n_pairs + p
    seq = pl.BlockSpec((1, t, HEAD_WIDTH), lambda b, h, p, pt: (b, 0, h))
    per_head = lambda a: pl.BlockSpec((1,) + a.shape[1:], lambda b, h, p, pt: (h, 0, 0))
    per_item = pl.BlockSpec((1, dec_seq, att_w), lambda b, h, p, pt: (item(b, h, p), 0, 0))

    in_hbm = pl.BlockSpec(memory_space=pl.ANY)
    in_specs = [pl.BlockSpec(memory_space=pltpu.SMEM), seq, seq, seq, per_head(near), per_head(mb),
                per_head(mm), _of_layer(g, layer), per_item, per_item, per_item,
                pl.BlockSpec(bias_s.shape, lambda b, h, p, pt: (0, 0, 0)), in_hbm, in_hbm]
    args = (lam, q, k, vb, near, mb, mm, g, qs, ks_new, vs_new, bias_s, cache_kt, cache_v)
    aliases = {}
    if kt_all is not None:
        in_specs.append(pl.BlockSpec(memory_space=pl.ANY))
        args += (kt_all,)
        aliases = {len(args): 1}
    grid_spec = pltpu.PrefetchScalarGridSpec(
        num_scalar_prefetch=1,
        grid=(bsz, n_heads, n_pairs),
        in_specs=in_specs,
        out_specs=[seq, pl.BlockSpec((None, None, HEAD_WIDTH, t),
                                     lambda b, h, p, pt: (layer, b, h, 0)), per_item],
        scratch_shapes=[pltpu.VMEM((t, HEAD_WIDTH), BF16), pltpu.VMEM((t, HEAD_WIDTH), BF16),
                        pltpu.VMEM((t, 2 * HEAD_WIDTH), BF16),
                        pltpu.VMEM((n_pages + 1, 2 * n_heads * dec_seq, PAGE_SIZE), F32),
                        pltpu.VMEM((2, n_pages) + cache_kt.shape[2:], F32),
                        pltpu.VMEM((2, n_pages) + cache_v.shape[2:], F32),
                        pltpu.SemaphoreType.DMA((2, 2))],
    )
    return pl.pallas_call(
        functools.partial(_attn_kernel, layer=layer, tq=tq, n_q=n_q, n_pages=n_pages,
                          n_heads=n_heads, dec_seq=dec_seq, sub_scale=sub_scale),
        grid_spec=grid_spec,
        out_shape=[jax.ShapeDtypeStruct(q.shape, BF16),
                   jax.ShapeDtypeStruct((depth, bsz, att_w, t), F32),
                   jax.ShapeDtypeStruct(qs.shape, F32)],
        input_output_aliases=aliases,
        compiler_params=pltpu.CompilerParams(
            dimension_semantics=("arbitrary", "arbitrary", "arbitrary"),
            vmem_limit_bytes=V7X_VMEM_LIMIT),
        name="attn",
    )(page_table.reshape(-1), *args)


def _ssm_kernel(u_ref, h0r_ref, h0i_ref, lbr_ref, lbi_ref, br_ref, bi_ref, cr_ref, ci_ref, d_ref,
                z_ref, hr_ref, hi_ref, u_tm, z_tm, *x_scratch, nb, tc):
    n_parts = u_tm.shape[0]
    part_s = lbr_ref.shape[1] // n_parts
    n_rows = nb * tc
    x_re, x_im = x_scratch[:n_parts], x_scratch[n_parts:]

    @pl.when(pl.program_id(0) == 0)
    def _():
        hr_ref[...] = h0r_ref[...]
        hi_ref[...] = h0i_ref[...]

    def to_time_major(b, c):
        ub = u_ref[b]
        for p in range(n_parts):
            u_tm[p, pl.ds(b, tc, stride=nb), :] = ub[:, p * LANES:(p + 1) * LANES]
        return c

    lax.fori_loop(0, nb, to_time_major, 0, unroll=(nb == 8))

    def project_in(p):
        up = u_tm[p].astype(BF16)
        x_re[p][...] = _dot(up, br_ref[p])
        x_im[p][...] = _dot(up, bi_ref[p])

    def recur(p):
        gcols = slice(p * part_s, (p + 1) * part_s)
        lr = lbr_ref[:, gcols]
        li = lbi_ref[:, gcols]

        def rows(r0):
            hr = hr_ref[pl.ds(r0, 8), gcols]
            hi = hi_ref[pl.ds(r0, 8), gcols]
            for t in range(tc):
                row = pl.ds(t * nb + r0, 8)
                hr, hi = (lr * hr - li * hi + x_re[p][row, :], lr * hi + li * hr + x_im[p][row, :])
                x_re[p][row, :] = hr
                x_im[p][row, :] = hi
            hr_ref[pl.ds(r0, 8), gcols] = hr
            hi_ref[pl.ds(r0, 8), gcols] = hi

        if nb == 8:
            rows(0)
        else:
            lax.fori_loop(0, nb // 8, lambda rg, c: (rows(pl.multiple_of(rg * 8, 8)), c)[1], 0)

    def project_out(p):
        rc = n_rows if n_rows <= 512 else 256
        for r0 in range(0, n_rows, rc):
            rows = slice(r0, r0 + rc)
            y = (_dot(x_re[p][rows, :].astype(BF16), cr_ref[p])
                 + _dot(x_im[p][rows, :].astype(BF16), ci_ref[p])
                 + d_ref[:, p * LANES:(p + 1) * LANES] * u_tm[p, rows, :])
            z_tm[p, rows, :] = jax.nn.gelu(y)

    for step in range(n_parts + 2):
        if step < n_parts:
            project_in(step)
        if 1 <= step <= n_parts:
            recur(step - 1)
        if step >= 2:
            project_out(step - 2)

    def to_batch_major(b, c):
        z_ref[b] = jnp.concatenate(
            [z_tm[p, pl.ds(b, tc, stride=nb), :] for p in range(n_parts)], axis=-1)
        return c

    lax.fori_loop(0, nb, to_batch_major, 0, unroll=(nb == 8))


def _ssm(u, h0_re, h0_im, tables, *, layer, tc):
    nb, t, ssm_w = u.shape
    n_state = tables[0].shape[-1]
    n_parts = ssm_w // LANES
    assert t % tc == 0 and nb % 8 == 0 and tables[2].shape[1:] == (n_parts, LANES, n_state // n_parts)
    state = pl.BlockSpec((nb, n_state), lambda i: (0, 0))
    seq = pl.BlockSpec((nb, tc, ssm_w), lambda i: (0, i, 0))
    return pl.pallas_call(
        functools.partial(_ssm_kernel, nb=nb, tc=tc),
        grid=(t // tc,),
        in_specs=[seq, state, state] + [_of_layer(a, layer) for a in tables],
        out_specs=[seq, state, state],
        out_shape=[jax.ShapeDtypeStruct(u.shape, F32), jax.ShapeDtypeStruct((nb, n_state), F32),
                   jax.ShapeDtypeStruct((nb, n_state), F32)],
        scratch_shapes=[pltpu.VMEM((n_parts, nb * tc, LANES), F32)] * 2
        + [pltpu.VMEM((nb * tc, n_state // n_parts), F32)] * (2 * n_parts),
        compiler_params=pltpu.CompilerParams(dimension_semantics=("arbitrary",),
                                             vmem_limit_bytes=V7X_VMEM_LIMIT),
        name="ssm",
    )(u, h0_re, h0_im, *tables)


def _merge_kernel(x_ref, on_ref, z_ref, wgate_ref, wau_ref, wglu_ref, wout_ref, g_ref, b_ref,
                  o_ref, *, alpha):
    d = x_ref.shape[1]
    x = x_ref[...]
    xb = x.astype(BF16)
    on = on_ref[...].astype(BF16)
    zb = z_ref[...].astype(BF16)

    def branch_matmuls(c0):
        cols = slice(c0, c0 + MERGE_CHUNK)
        gcols = slice(d + c0, d + c0 + MERGE_CHUNK)
        return (_dot(xb, wgate_ref[:, cols]), _dot(xb, wgate_ref[:, gcols]),
                _dot(on, wau_ref[:, cols]), _dot(zb, wglu_ref[:, cols]), _dot(zb, wglu_ref[:, gcols]))

    def mix_out(c0, parts):
        ga, gb, a_branch, glu_v, glu_g = parts
        b_branch = glu_v * jax.nn.sigmoid(glu_g)
        mix_in = jax.nn.sigmoid(ga) * a_branch + jax.nn.sigmoid(gb) * b_branch
        return _dot(mix_in.astype(BF16), wout_ref[c0:c0 + MERGE_CHUNK, :])

    acc = alpha * x
    pending = None
    for c0 in range(0, d, MERGE_CHUNK):
        parts = branch_matmuls(c0)
        if pending is not None:
            acc = acc + mix_out(*pending)
        pending = (c0, parts)
    acc = acc + mix_out(*pending)
    o_ref[...] = _layer_norm(acc, g_ref[...], b_ref[...])


def _merge(x, o_n, z, w_gate, w_att_up, w_glu, w_out, ln_g, ln_b, *, layer, tm, alpha):
    n, d = x.shape
    row = lambda a: pl.BlockSpec((tm, a.shape[1]), lambda i: (i, 0))
    params = (w_gate, w_att_up, w_glu, w_out, ln_g, ln_b)
    return pl.pallas_call(
        functools.partial(_merge_kernel, alpha=alpha),
        grid=(n // tm,),
        in_specs=[row(x), row(o_n), row(z)] + [_of_layer(a, layer) for a in params],
        out_specs=row(x),
        out_shape=jax.ShapeDtypeStruct(x.shape, F32),
        compiler_params=pltpu.CompilerParams(dimension_semantics=("parallel",),
                                             vmem_limit_bytes=V7X_VMEM_LIMIT),
        name="merge",
    )(x, o_n, z, w_gate, w_att_up, w_glu, w_out, ln_g, ln_b)


def _ffn_kernel(x_ref, wup_ref, wdn_ref, g_ref, b_ref, o_ref, *, alpha, ff_chunk):
    x = x_ref[...]
    xb = x.astype(BF16)
    acc = alpha * x
    for c0 in range(0, wup_ref.shape[1], ff_chunk):
        hid = jnp.maximum(_dot(xb, wup_ref[:, c0:c0 + ff_chunk]), 0.0)
        acc = acc + _dot((hid * hid).astype(BF16), wdn_ref[c0:c0 + ff_chunk, :])
    o_ref[...] = _layer_norm(acc, g_ref[...], b_ref[...])


def _ffn(x, w_up, w_down, ln_g, ln_b, *, layer, tm, alpha):
    n, d = x.shape
    row = pl.BlockSpec((tm, d), lambda i: (i, 0))
    return pl.pallas_call(
        functools.partial(_ffn_kernel, alpha=alpha, ff_chunk=d),
        grid=(n // tm,),
        in_specs=[row] + [_of_layer(a, layer) for a in (w_up, w_down, ln_g, ln_b)],
        out_specs=row,
        out_shape=jax.ShapeDtypeStruct(x.shape, F32),
        compiler_params=pltpu.CompilerParams(dimension_semantics=("parallel",),
                                             vmem_limit_bytes=V7X_VMEM_LIMIT),
        name="ffn",
    )(x, w_up, w_down, ln_g, ln_b)


def _ffn_skip_meta(x, w_up, w_down, ln_g, ln_b, *, layer, tm, alpha):
    bsz, t, d = x.shape
    seq = t - N_META
    assert seq % tm == 0
    window = pl.BlockSpec((pl.Squeezed(), pl.Element(tm), pl.Element(d)),
                          lambda b, j: (b, pl.multiple_of(N_META + j * tm, N_META), 0))
    return pl.pallas_call(
        functools.partial(_ffn_kernel, alpha=alpha, ff_chunk=d),
        grid=(bsz, seq // tm),
        in_specs=[window] + [_of_layer(a, layer) for a in (w_up, w_down, ln_g, ln_b)],
        out_specs=pl.BlockSpec((None, tm, d), lambda b, j: (b, j, 0)),
        out_shape=jax.ShapeDtypeStruct((bsz, seq, d), F32),
        compiler_params=pltpu.CompilerParams(dimension_semantics=("parallel", "parallel"),
                                             vmem_limit_bytes=V7X_VMEM_LIMIT),
        name="ffn_skip_meta",
    )(x, w_up, w_down, ln_g, ln_b)


def _ssm_tables(a_re, a_im, log_dt, b_re, b_im, c_re, c_im, d):
    depth, n_groups, n_p = a_re.shape
    a_re = a_re.astype(F32)
    a_im = a_im.astype(F32)
    dt = jnp.exp(log_dt.astype(F32))[..., None]
    mag = jnp.exp(a_re * dt)
    lb_re = mag * jnp.cos(a_im * dt)
    lb_im = mag * jnp.sin(a_im * dt)
    nr, ni = lb_re - 1.0, lb_im
    den = a_re * a_re + a_im * a_im
    f_re = (nr * a_re + ni * a_im) / den
    f_im = (ni * a_re - nr * a_im) / den
    b_re = b_re.astype(F32)
    b_im = b_im.astype(F32)
    bb_re = f_re[..., None] * b_re - f_im[..., None] * b_im
    bb_im = f_re[..., None] * b_im + f_im[..., None] * b_re
    gpp = LANES // GROUP_CH
    n_parts = n_groups // gpp
    eye = jnp.eye(gpp, dtype=F32)

    def in_mat(bb):
        bb = bb.reshape(depth, n_parts, gpp, n_p, GROUP_CH)
        full = jnp.einsum('lngpc,gh->lngchp', bb, eye)
        return full.reshape(depth, n_parts, LANES, gpp * n_p).astype(BF16)

    def out_mat(cc):
        cc = cc.reshape(depth, n_parts, gpp, GROUP_CH, n_p)
        full = jnp.einsum('lngcp,gh->lngphc', cc, eye)
        return full.reshape(depth, n_parts, gpp * n_p, LANES).astype(BF16)

    sublanes = lambda a: jnp.broadcast_to(a.reshape(depth, 1, -1), (depth, 8, n_groups * n_p))
    return (sublanes(lb_re), sublanes(lb_im), in_mat(bb_re), in_mat(bb_im),
            out_mat(c_re.astype(F32)), out_mat(-c_im.astype(F32)), d.astype(F32).reshape(depth, 1, -1))


def kernel(x_prompt, x_sample, cache_k, cache_v, state_ssm_re, state_ssm_im, page_table, meta_tokens, ln_in_g, ln_in_b, rel_bias, w_in, lambda_q1, lambda_k1, lambda_q2, lambda_k2, subln_g, w_att_up, ssm_a_re, ssm_a_im, ssm_log_dt, ssm_b_re, ssm_b_im, ssm_c_re, ssm_c_im, ssm_d, w_glu, w_out, ln1_g, ln1_b, w_up, w_down, ln2_g, ln2_b):
    bsz, seq, d_model = x_prompt.shape
    dec_b, dec_seq, _ = x_sample.shape
    depth = w_in.shape[0]
    att_w = w_att_up.shape[1]
    ssm_w = w_glu.shape[1]
    n_heads = att_w // HEAD_WIDTH
    n_groups = ssm_w // GROUP_CH
    n_state = n_groups * STATE_DIM
    t = N_META + seq
    past = page_table.shape[1] * PAGE_SIZE
    alpha = (2.0 * depth) ** 0.25
    assert seq % ATT_Q_TILE == 0 and t % SSM_CHUNK == 0 and past >= 2 * PAGE_SIZE

    tm_p = _row_tile(t, 704)
    tm_s = _row_tile(dec_b * dec_seq, 512)

    xp = x_prompt
    kt_all = v_all = None
    xs = x_sample.reshape(dec_b * dec_seq, d_model)
    row = lambda a: a.astype(F32).reshape(1, -1)
    near, mb, mm, bias_s = _bias_tables(rel_bias, ATT_Q_TILE, dec_seq)
    n_pool = cache_k.shape[1]
    cache_kt = jnp.transpose(cache_k, (0, 1, 3, 4, 5, 2)).reshape(depth, n_pool, att_w, PAGE_SIZE)
    cache_v = cache_v.reshape(depth, n_pool, PAGE_SIZE * n_heads, HEAD_WIDTH)
    h0p = jnp.zeros((bsz, n_state), F32)

    lam_init = [0.8 - 0.6 * math.exp(-0.3 * l) for l in range(depth)]
    f32 = lambda a: a.astype(F32)
    lam = (jnp.exp(jnp.sum(f32(lambda_q1) * f32(lambda_k1), -1))
           - jnp.exp(jnp.sum(f32(lambda_q2) * f32(lambda_k2), -1)) + jnp.asarray(lam_init, F32))
    rows = lambda a: a.astype(F32).reshape(depth, 1, -1)
    g_sub = rows(subln_g)
    n_qkvu = 3 * att_w + ssm_w
    w_qkvu_bf = w_in[:, :, :n_qkvu].astype(BF16)
    w_gate_bf = w_in[:, :, n_qkvu:].astype(BF16)
    merge_params = (w_gate_bf, w_att_up.astype(BF16), w_glu.astype(BF16), w_out.astype(BF16),
                    rows(ln1_g), rows(ln1_b))
    ffn_params = (w_up.astype(BF16), w_down.astype(BF16), rows(ln2_g), rows(ln2_b))
    ssm_tables = _ssm_tables(ssm_a_re, ssm_a_im, ssm_log_dt, ssm_b_re, ssm_b_im, ssm_c_re, ssm_c_im,
                             ssm_d)
    state_re = state_ssm_re.reshape(depth, dec_b, n_state).astype(F32)
    state_im = state_ssm_im.reshape(depth, dec_b, n_state).astype(F32)

    outs = {name: [] for name in ("hpr", "hpi", "ks", "vs", "hsr", "hsi")}
    for l in range(depth):
        sub_scale = 1.0 - lam_init[l]

        *x_ln, q, kf, vb, u, v_all = _in_proj_prompt(
            xp, meta_tokens.astype(F32), row(ln_in_g), row(ln_in_b), w_qkvu_bf, v_all, layer=l,
            depth=depth, bsz=bsz, t=t, tm=tm_p, att_w=att_w, ssm_w=ssm_w)
        xp = x_ln[0] if x_ln else xp
        *x_ln, qs, ks_new, vs_new, us = _in_proj(xs, row(ln_in_g), row(ln_in_b), w_qkvu_bf, layer=l,
                                                 tm=tm_s, att_w=att_w, ssm_w=ssm_w)
        xs = x_ln[0] if x_ln else xs
        seq3 = lambda a: a.reshape(bsz, t, a.shape[-1])
        dec3 = lambda a: a.reshape(dec_b, dec_seq, a.shape[-1])
        o_n, kt_all, os_n = _attn(page_table, lam, seq3(q), seq3(kf), seq3(vb), near, mb, mm, g_sub,
                                  kt_all, dec3(qs), dec3(ks_new), dec3(vs_new), bias_s,
                                  cache_kt, cache_v, layer=l, depth=depth, sub_scale=sub_scale)

        z, hpr, hpi = _ssm(seq3(u), h0p, h0p, ssm_tables, layer=l, tc=SSM_CHUNK)
        xp = _merge(xp, o_n.reshape(bsz * t, att_w), z.reshape(bsz * t, ssm_w), *merge_params,
                    layer=l, tm=tm_p, alpha=alpha)
        if l < depth - 1:
            xp = _ffn(xp, *ffn_params, layer=l, tm=tm_p, alpha=alpha)
        else:
            y_prompt = _ffn_skip_meta(xp.reshape(bsz, t, d_model), *ffn_params, layer=l,
                                      tm=_row_tile(seq, 512), alpha=alpha)
        outs["hpr"].append(hpr.reshape(bsz, n_groups, STATE_DIM))
        outs["hpi"].append(hpi.reshape(bsz, n_groups, STATE_DIM))

        z, hsr, hsi = _ssm(dec3(us), state_re[l], state_im[l], ssm_tables, layer=l, tc=dec_seq)
        xs = _merge(xs, os_n.reshape(dec_b * dec_seq, att_w), z.reshape(dec_b * dec_seq, ssm_w),
                    *merge_params, layer=l, tm=tm_s, alpha=alpha)
        xs = _ffn(xs, *ffn_params, layer=l, tm=tm_s, alpha=alpha)
        outs["ks"].append(ks_new.reshape(dec_b, dec_seq, n_heads, 2, HEAD_DIM))
        outs["vs"].append(vs_new.reshape(dec_b, dec_seq, n_heads, HEAD_WIDTH))
        outs["hsr"].append(hsr.reshape(dec_b, n_groups, STATE_DIM))
        outs["hsi"].append(hsi.reshape(dec_b, n_groups, STATE_DIM))

    y_sample = xs.reshape(dec_b, dec_seq, d_model)
    st = lambda name: jnp.stack(outs[name])
    k_prompt = jnp.transpose(kt_all.reshape(depth, bsz, n_heads, 2, HEAD_DIM, t), (0, 1, 5, 2, 3, 4))
    v_prompt = v_all.reshape(depth, bsz, t, n_heads, HEAD_WIDTH)
    return (y_prompt, y_sample, k_prompt, v_prompt, st("hpr"), st("hpi"), st("ks"), st("vs"),
            st("hsr"), st("hsi"))
```

```python
import functools
import math

import jax
import jax.numpy as jnp
from jax import lax
from jax.experimental import pallas as pl
from jax.experimental.pallas import tpu as pltpu

F32 = jnp.float32
BF16 = jnp.bfloat16

N_META = 16
HEAD_DIM = 64
HEAD_WIDTH = 2 * HEAD_DIM
GROUP_CH = 16
STATE_DIM = 64
PAGE_SIZE = 128
LANES = 128
N_BUCKETS = 32
MAX_DISTANCE = 128
EPS = 1e-5
LOG2E = math.log2(math.e)

ATT_Q_TILE = 256
MERGE_CHUNK = 256
SSM_CHUNK = 48
V7X_VMEM_BYTES = 64 * 1024 * 1024
V7X_VMEM_LIMIT = V7X_VMEM_BYTES * 7 // 8

assert (N_BUCKETS // 2 + int(math.log((MAX_DISTANCE + 1) / (N_BUCKETS // 2))
                             / math.log(MAX_DISTANCE / (N_BUCKETS // 2)) * (N_BUCKETS // 2))
        >= N_BUCKETS - 1)


def _layer_norm(x, g, b):
    mu = jnp.mean(x, -1, keepdims=True)
    xc = x - mu
    var = jnp.mean(xc * xc, -1, keepdims=True)
    return xc * lax.rsqrt(var + EPS) * g + b


def _dot(a, b):
    return jnp.dot(a, b, preferred_element_type=F32)


def _dot_nt(a, b):
    return lax.dot_general(a, b, (((1,), (1,)), ((), ())), preferred_element_type=F32)


def _resident(shape):
    return pl.BlockSpec(shape, lambda *_: (0,) * len(shape), pipeline_mode=pl.Buffered(1))


def _of_layer(a, layer):
    return pl.BlockSpec((None,) + a.shape[1:], lambda *_: (layer,) + (0,) * (a.ndim - 1),
                        pipeline_mode=pl.Buffered(1))


def _row_tile(n_rows, target):
    best = None
    for t in range(16, min(n_rows, target) + 1, 16):
        if n_rows % t == 0:
            best = t
    assert best is not None, n_rows
    return best


def _project_qkvu(x, w_ref, att_w, ssm_w):
    xb = x.astype(BF16)
    widths = (att_w, att_w, att_w, ssm_w)
    starts = [sum(widths[:i]) for i in range(len(widths))]
    q, k, v, u = (_dot(xb, w_ref[:, c0:c0 + w]) for c0, w in zip(starts, widths))
    return (q * (HEAD_DIM ** -0.5 * LOG2E)).astype(BF16), k, v, u


def _in_proj_kernel(x_ref, g_ref, b_ref, w_ref, *out_refs, apply_ln, att_w, ssm_w):
    x = x_ref[...]
    if apply_ln:
        xo_ref, *out_refs = out_refs
        x = _layer_norm(x, g_ref[...], b_ref[...])
        xo_ref[...] = x
    q_ref, k_ref, v_ref, u_ref = out_refs
    q, k_ref[...], v_ref[...], u_ref[...] = _project_qkvu(x, w_ref, att_w, ssm_w)
    q_ref[...] = q.astype(F32)


def _in_proj(x, ln_g, ln_b, w_qkvu, *, layer, tm, att_w, ssm_w):
    n, d = x.shape
    apply_ln = layer == 0
    row = lambda w: pl.BlockSpec((tm, w), lambda i: (i, 0))
    outs = [(att_w, F32), (att_w, F32), (att_w, F32), (ssm_w, F32)]
    if apply_ln:
        outs = [(d, F32)] + outs
    return pl.pallas_call(
        functools.partial(_in_proj_kernel, apply_ln=apply_ln, att_w=att_w, ssm_w=ssm_w),
        grid=(n // tm,),
        in_specs=[row(d), _resident((1, d)), _resident((1, d)), _of_layer(w_qkvu, layer)],
        out_specs=[row(w) for w, _ in outs],
        out_shape=[jax.ShapeDtypeStruct((n, w), dt) for w, dt in outs],
        compiler_params=pltpu.CompilerParams(dimension_semantics=("parallel",),
                                             vmem_limit_bytes=V7X_VMEM_LIMIT),
        name="in_proj",
    )(x, ln_g, ln_b, w_qkvu)


def _in_proj_prompt_kernel(*refs, first, tm, att_w, ssm_w):
    if first:
        x_ref, meta_ref, g_ref, b_ref, w_ref, xo_ref, q_ref, k_ref, vb_ref, u_ref, vall_ref = refs
    else:
        x_ref, w_ref, _, q_ref, k_ref, vb_ref, u_ref, vall_ref = refs
    n_heads = att_w // HEAD_WIDTH

    def body(x):
        if first:
            x = _layer_norm(x, g_ref[...], b_ref[...])
            xo_ref[...] = x
        q_ref[...], k_ref[...], v, u_ref[...] = _project_qkvu(x, w_ref, att_w, ssm_w)
        vb_ref[...] = v.astype(BF16)
        for h in range(n_heads):
            vall_ref[pl.ds(h, tm, stride=n_heads), :] = v[:, h * HEAD_WIDTH:(h + 1) * HEAD_WIDTH]

    if first:
        @pl.when(pl.program_id(1) == 0)
        def _():
            body(jnp.concatenate([meta_ref[...], x_ref[:tm - N_META, :]], axis=0))

        @pl.when(pl.program_id(1) != 0)
        def _():
            body(x_ref[...])
    else:
        body(x_ref[...])


def _in_proj_prompt(x, meta, ln_g, ln_b, w_qkvu, v_all, *, layer, depth, bsz, t, tm, att_w, ssm_w):
    first = layer == 0
    d = x.shape[-1]
    n = bsz * t
    n_j = t // tm
    n_heads = att_w // HEAD_WIDTH
    row = lambda w: pl.BlockSpec((tm, w), lambda b, j: (b * n_j + j, 0))
    outs = [(att_w, BF16), (att_w, F32), (att_w, BF16), (ssm_w, F32)]
    if first:
        window = pl.BlockSpec(
            (pl.Squeezed(), pl.Element(tm), pl.Element(d)),
            lambda b, j: (b, pl.multiple_of(jnp.maximum(j * tm - N_META, 0), N_META), 0))
        in_specs = [window, _resident(meta.shape), _resident((1, d)), _resident((1, d)),
                    _of_layer(w_qkvu, layer)]
        args = (x, meta, ln_g, ln_b, w_qkvu)
        outs = [(d, F32)] + outs
        aliases = {}
    else:
        in_specs = [row(d), _of_layer(w_qkvu, layer), pl.BlockSpec(memory_space=pl.ANY)]
        args = (x, w_qkvu, v_all)
        aliases = {2: len(outs)}
    vall_shape = (depth, n * n_heads, HEAD_WIDTH)
    return pl.pallas_call(
        functools.partial(_in_proj_prompt_kernel, first=first, tm=tm, att_w=att_w, ssm_w=ssm_w),
        grid=(bsz, n_j),
        in_specs=in_specs,
        out_specs=[row(w) for w, _ in outs]
        + [pl.BlockSpec((None, tm * n_heads, HEAD_WIDTH), lambda b, j: (layer, b * n_j + j, 0))],
        out_shape=[jax.ShapeDtypeStruct((n, w), dt) for w, dt in outs]
        + [jax.ShapeDtypeStruct(vall_shape, F32)],
        input_output_aliases=aliases,
        compiler_params=pltpu.CompilerParams(dimension_semantics=("parallel", "parallel"),
                                             vmem_limit_bytes=V7X_VMEM_LIMIT),
        name="in_proj_prompt",
    )(*args)


def _bias_kernel(rb_ref, rel_ref, o_ref):
    rel = rel_ref[...]
    n = jnp.maximum(rel, 0)
    exact = N_BUCKETS // 2
    large = exact + (jnp.log(jnp.maximum(n, 1).astype(F32) / exact)
                     / math.log(MAX_DISTANCE / exact) * (N_BUCKETS - exact)).astype(jnp.int32)
    bucket = jnp.where(n < exact, n, jnp.minimum(large, N_BUCKETS - 1))
    for h in range(o_ref.shape[0]):
        far = rb_ref[N_BUCKETS - 1, h]
        acc = jnp.zeros(rel.shape, F32)
        for b in range(N_BUCKETS - 1):
            acc = jnp.where(bucket == b, (rb_ref[b, h] - far) * LOG2E, acc)
        o_ref[h] = jnp.where(rel >= 0, acc, -jnp.inf)


def _bias_tile(rel_bias, rel):
    n_heads = rel_bias.shape[1]
    return pl.pallas_call(
        _bias_kernel,
        in_specs=[pl.BlockSpec(memory_space=pltpu.SMEM), pl.BlockSpec(rel.shape, lambda: (0, 0))],
        out_specs=pl.BlockSpec((n_heads,) + rel.shape, lambda: (0, 0, 0)),
        out_shape=jax.ShapeDtypeStruct((n_heads,) + rel.shape, F32),
        name="bias_tile",
    )(rel_bias.astype(F32), rel)


def _bias_tables(rel_bias, tq, dec_seq):
    i32 = jnp.int32
    n_heads = rel_bias.shape[1]
    r = jnp.arange(tq, dtype=i32)[:, None]
    near = _bias_tile(rel_bias, tq + r - jnp.arange(2 * tq, dtype=i32)[None, :])
    mb = _bias_tile(rel_bias, N_META + r - jnp.arange(N_META, dtype=i32)[None, :])
    mr = jnp.arange(N_META, dtype=i32)
    mm = _bias_tile(rel_bias, mr[:, None] - mr[None, :])
    s = jnp.arange(dec_seq, dtype=i32)[:, None]
    kk = jnp.arange(PAGE_SIZE, dtype=i32)[None, :]
    last = _bias_tile(rel_bias, PAGE_SIZE + s - kk)
    new = _bias_tile(rel_bias, jnp.where(kk < dec_seq, s - kk, -1))

    def rows(tile):
        return jnp.broadcast_to(tile[:, None], (n_heads, 2, dec_seq, PAGE_SIZE)).reshape(-1, PAGE_SIZE)

    return near, mb, mm, jnp.stack([rows(last), rows(new)])


def _softmax_pv(blocks):
    m = functools.reduce(jnp.maximum, [jnp.max(s, -1, keepdims=True) for s, _ in blocks])
    acc = 0.0
    for s, v_ones in blocks:
        acc = acc + _dot(jnp.exp2(s - m).astype(BF16), v_ones)
    return acc[:, :HEAD_WIDTH] / acc[:, HEAD_WIDTH:HEAD_WIDTH + 1]


def _sub_norm(o1, o2, lam, g, sub_scale):
    o = o1 - lam * o2
    return o * lax.rsqrt(jnp.mean(o * o, -1, keepdims=True) + EPS) * g * sub_scale


def _sample_scores(q_ref, kn_ref, bias_ref, kt_pages, s_ref, *, n_heads, dec_seq):
    n_pages = len(kt_pages)
    att_w = q_ref.shape[2]
    n_rows = 2 * n_heads * dec_seq
    q_rep = jnp.concatenate([q_ref[0]] * (2 * n_heads), axis=0)
    row_blk = lax.broadcasted_iota(jnp.int32, (n_rows, att_w), 0) // dec_seq
    col_blk = lax.broadcasted_iota(jnp.int32, (n_rows, att_w), 1) // HEAD_DIM
    q_exp = jnp.where(row_blk == col_blk, q_rep, 0.0).astype(BF16)
    pad = jnp.zeros((PAGE_SIZE - dec_seq, att_w), F32)
    k_new = jnp.concatenate([kn_ref[0], pad], axis=0).astype(BF16)
    m_el = None
    for p in range(n_pages + 1):
        if p < n_pages:
            s = _dot(q_exp, kt_pages[p][...].astype(BF16))
            if p == n_pages - 1:
                s = s + bias_ref[0]
        else:
            s = _dot_nt(q_exp, k_new) + bias_ref[1]
        s_ref[p] = s
        m_el = s if m_el is None else jnp.maximum(m_el, s)
    return jnp.max(m_el, -1, keepdims=True)


def _sample_values(m, s_ref, vn_ref, v_pages, lam, g, *, n_heads, dec_seq, sub_scale):
    n_pages = len(v_pages)
    att_w = vn_ref.shape[2]
    head_rows = 2 * dec_seq
    pad = jnp.zeros((PAGE_SIZE - dec_seq, att_w), F32)
    v_new = jnp.concatenate([vn_ref[0], pad], axis=0).astype(BF16)
    l_el = jnp.zeros(s_ref.shape[1:], F32)
    accs = [jnp.zeros((head_rows, HEAD_WIDTH), F32) for _ in range(n_heads)]
    for p in range(n_pages + 1):
        e = jnp.exp2(s_ref[p] - m)
        l_el = l_el + e
        eb = e.astype(BF16)
        for h in range(n_heads):
            if p < n_pages:
                v_h = v_pages[p][pl.ds(h, PAGE_SIZE, stride=n_heads), :].astype(BF16)
            else:
                v_h = v_new[:, h * HEAD_WIDTH:(h + 1) * HEAD_WIDTH]
            accs[h] = accs[h] + _dot(eb[h * head_rows:(h + 1) * head_rows, :], v_h)
    l = jnp.sum(l_el, -1, keepdims=True)
    heads = []
    for h in range(n_heads):
        o = accs[h] / l[h * head_rows:(h + 1) * head_rows]
        heads.append(_sub_norm(o[:dec_seq], o[dec_seq:], lam, g, sub_scale))
    return jnp.concatenate(heads, axis=-1)


def _attn_kernel(pt_ref, lam_ref, q_ref, k_ref, v_ref, near_ref, mb_ref, mm_ref, g_ref,
                 qs_ref, kn_ref, vn_ref, bias_ref, *rest, layer, tq, n_q, n_pages, n_heads, dec_seq,
                 sub_scale):
    ckt_ref, cv_ref = rest[:2]
    (o_ref, kt_ref, os_ref, k1_ref, k2_ref, v1_ref, s_ref, kbuf, vbuf, sem) = rest[-10:]
    pair = pl.program_id(2)
    n_pairs = pl.num_programs(2)
    item = (pl.program_id(0) * n_heads + pl.program_id(1)) * n_pairs + pair
    n_items = pl.num_programs(0) * n_heads * n_pairs
    slot = item % 2

    def page_copies(it, sl):
        copies = []
        for pg in range(n_pages):
            idx = pt_ref[it * n_pages + pg]
            copies.append(pltpu.make_async_copy(ckt_ref.at[layer, idx], kbuf.at[sl, pg], sem.at[0, sl]))
            copies.append(pltpu.make_async_copy(cv_ref.at[layer, idx], vbuf.at[sl, pg], sem.at[1, sl]))
        return copies

    @pl.when(item == 0)
    def _():
        for c in page_copies(0, 0):
            c.start()

    @pl.when(item + 1 < n_items)
    def _():
        for c in page_copies(item + 1, 1 - slot):
            c.start()

    for c in page_copies(item, slot):
        c.wait()
    kt_pages = [kbuf.at[slot, pg] for pg in range(n_pages)]
    v_pages = [vbuf.at[slot, pg] for pg in range(n_pages)]
    lam = lam_ref[layer]
    g = g_ref[...]
    t = k_ref.shape[1]

    def scores(rows, spans):
        q_t = q_ref[0, rows, :]
        comps = []
        for kc_ref in (k1_ref, k2_ref):
            blocks = []
            for k0, size, bias in spans:
                s = _dot_nt(q_t, kc_ref[k0:k0 + size, :])
                blocks.append((s if bias is None else s + bias, v1_ref[k0:k0 + size, :]))
            comps.append(blocks)
        return comps

    def finish(rows, comps):
        outs = [_softmax_pv(blocks) for blocks in comps]
        o_ref[0, rows, :] = _sub_norm(outs[0], outs[1], lam, g, sub_scale).astype(BF16)

    @pl.when(pair == 0)
    def _():
        k = k_ref[0]
        lane = lax.broadcasted_iota(jnp.int32, k.shape, 1)
        k1_ref[...] = jnp.where(lane < HEAD_DIM, k, 0.0).astype(BF16)
        k2_ref[...] = jnp.where(lane >= HEAD_DIM, k, 0.0).astype(BF16)
        v1_ref[:, :HEAD_WIDTH] = v_ref[0]
        v1_ref[:, HEAD_WIDTH:] = (lane == 0).astype(F32).astype(BF16)
        n_full = t // LANES
        for c in range(n_full):
            kt_ref[:, c * LANES:(c + 1) * LANES] = k_ref[0, c * LANES:(c + 1) * LANES, :].T
        tail = t - n_full * LANES
        if tail:
            k_tail = jnp.concatenate([k_ref[0, n_full * LANES:, :],
                                      jnp.zeros((LANES - tail, HEAD_WIDTH), F32)], axis=0)
            kt_ref[:, n_full * LANES:] = k_tail.T[:, :tail]
        meta_rows = slice(0, N_META)
        finish(meta_rows, scores(meta_rows, [(0, N_META, mm_ref[0])]))

    def tile(i):
        r0 = N_META + i * tq
        if i == 0:
            spans = [(0, N_META, mb_ref[0]), (r0, tq, near_ref[0, :, tq:])]
        else:
            spans = [(0, r0 - tq, None), (r0 - tq, 2 * tq, near_ref[0])]
        return slice(r0, r0 + tq), spans

    sample = dict(n_heads=n_heads, dec_seq=dec_seq)
    for pr in range(n_q // 2):
        @pl.when(pair == pr)
        def _(pr=pr):
            first, second = tile(pr), tile(n_q - 1 - pr)
            m = _sample_scores(qs_ref, kn_ref, bias_ref, kt_pages, s_ref, **sample)
            comps_first = scores(*first)
            comps_second = scores(*second)
            os_ref[0] = _sample_values(m, s_ref, vn_ref, v_pages, lam, g, sub_scale=sub_scale,
                                       **sample)
            finish(first[0], comps_first)
            finish(second[0], comps_second)


def _attn(page_table, lam, q, k, vb, near, mb, mm, g, kt_all, qs, ks_new, vs_new, bias_s, cache_kt,
          cache_v, *, layer, depth, sub_scale):
    bsz, t, att_w = q.shape
    dec_b, dec_seq, _ = qs.shape
    n_pages = page_table.shape[1]
    n_heads = att_w // HEAD_WIDTH
    tq = near.shape[1]
    n_q = (t - N_META) // tq
    n_pairs = n_q // 2
    assert n_q * tq + N_META == t and tq > MAX_DISTANCE and n_q % 2 == 0
    assert dec_b == bsz * n_heads * n_pairs
    item = lambda b, h, p: (b * n_heads + h) * n_pairs + p
    seq = pl.BlockSpec((1, t, HEAD_WIDTH), lambda b, h, p, pt: (b, 0, h))
    per_head = lambda a: pl.BlockSpec((1,) + a.shape[1:], lambda b, h, p, pt: (h, 0, 0))
    per_item = pl.BlockSpec((1, dec_seq, att_w), lambda b, h, p, pt: (item(b, h, p), 0, 0))

    in_hbm = pl.BlockSpec(memory_space=pl.ANY)
    in_specs = [pl.BlockSpec(memory_space=pltpu.SMEM), seq, seq, seq, per_head(near), per_head(mb),
                per_head(mm), _of_layer(g, layer), per_item, per_item, per_item,
                pl.BlockSpec(bias_s.shape, lambda b, h, p, pt: (0, 0, 0)), in_hbm, in_hbm]
    args = (lam, q, k, vb, near, mb, mm, g, qs, ks_new, vs_new, bias_s, cache_kt, cache_v)
    aliases = {}
    if kt_all is not None:
        in_specs.append(pl.BlockSpec(memory_space=pl.ANY))
        args += (kt_all,)
        aliases = {len(args): 1}
    grid_spec = pltpu.PrefetchScalarGridSpec(
        num_scalar_prefetch=1,
        grid=(bsz, n_heads, n_pairs),
        in_specs=in_specs,
        out_specs=[seq, pl.BlockSpec((None, None, HEAD_WIDTH, t),
                                     lambda b, h, p, pt: (layer, b, h, 0)), per_item],
        scratch_shapes=[pltpu.VMEM((t, HEAD_WIDTH), BF16), pltpu.VMEM((t, HEAD_WIDTH), BF16),
                        pltpu.VMEM((t, 2 * HEAD_WIDTH), BF16),
                        pltpu.VMEM((n_pages + 1, 2 * n_heads * dec_seq, PAGE_SIZE), F32),
                        pltpu.VMEM((2, n_pages) + cache_kt.shape[2:], F32),
                        pltpu.VMEM((2, n_pages) + cache_v.shape[2:], F32),
                        pltpu.SemaphoreType.DMA((2, 2))],
    )
    return pl.pallas_call(
        functools.partial(_attn_kernel, layer=layer, tq=tq, n_q=n_q, n_pages=n_pages,
                          n_heads=n_heads, dec_seq=dec_seq, sub_scale=sub_scale),
        grid_spec=grid_spec,
        out_shape=[jax.ShapeDtypeStruct(q.shape, BF16),
                   jax.ShapeDtypeStruct((depth, bsz, att_w, t), F32),
                   jax.ShapeDtypeStruct(qs.shape, F32)],
        input_output_aliases=aliases,
        compiler_params=pltpu.CompilerParams(
            dimension_semantics=("arbitrary", "arbitrary", "arbitrary"),
            vmem_limit_bytes=V7X_VMEM_LIMIT),
        name="attn",
    )(page_table.reshape(-1), *args)


def _ssm_kernel(u_ref, h0r_ref, h0i_ref, lbr_ref, lbi_ref, br_ref, bi_ref, cr_ref, ci_ref, d_ref,
                z_ref, hr_ref, hi_ref, u_tm, z_tm, *x_scratch, nb, tc):
    n_parts = u_tm.shape[0]
    part_s = lbr_ref.shape[1] // n_parts
    n_rows = nb * tc
    x_re, x_im = x_scratch[:n_parts], x_scratch[n_parts:]

    @pl.when(pl.program_id(0) == 0)
    def _():
        hr_ref[...] = h0r_ref[...]
        hi_ref[...] = h0i_ref[...]

    def to_time_major(b, c):
        ub = u_ref[b]
        for p in range(n_parts):
            u_tm[p, pl.ds(b, tc, stride=nb), :] = ub[:, p * LANES:(p + 1) * LANES]
        return c

    lax.fori_loop(0, nb, to_time_major, 0, unroll=(nb == 8))

    def project_in(p):
        up = u_tm[p].astype(BF16)
        x_re[p][...] = _dot(up, br_ref[p])
        x_im[p][...] = _dot(up, bi_ref[p])

    def recur(p):
        gcols = slice(p * part_s, (p + 1) * part_s)
        lr = lbr_ref[:, gcols]
        li = lbi_ref[:, gcols]

        def rows(r0):
            hr = hr_ref[pl.ds(r0, 8), gcols]
            hi = hi_ref[pl.ds(r0, 8), gcols]
            for t in range(tc):
                row = pl.ds(t * nb + r0, 8)
                hr, hi = (lr * hr - li * hi + x_re[p][row, :], lr * hi + li * hr + x_im[p][row, :])
                x_re[p][row, :] = hr
                x_im[p][row, :] = hi
            hr_ref[pl.ds(r0, 8), gcols] = hr
            hi_ref[pl.ds(r0, 8), gcols] = hi

        if nb == 8:
            rows(0)
        else:
            lax.fori_loop(0, nb // 8, lambda rg, c: (rows(pl.multiple_of(rg * 8, 8)), c)[1], 0)

    def project_out(p):
        rc = n_rows if n_rows <= 512 else 256
        for r0 in range(0, n_rows, rc):
            rows = slice(r0, r0 + rc)
            y = (_dot(x_re[p][rows, :].astype(BF16), cr_ref[p])
                 + _dot(x_im[p][rows, :].astype(BF16), ci_ref[p])
                 + d_ref[:, p * LANES:(p + 1) * LANES] * u_tm[p, rows, :])
            z_tm[p, rows, :] = jax.nn.gelu(y)

    for step in range(n_parts + 1):
        if step < n_parts:
            project_in(step)
        if step >= 1:
            recur(step - 1)
            project_out(step - 1)

    def to_batch_major(b, c):
        z_ref[b] = jnp.concatenate(
            [z_tm[p, pl.ds(b, tc, stride=nb), :] for p in range(n_parts)], axis=-1)
        return c

    lax.fori_loop(0, nb, to_batch_major, 0, unroll=(nb == 8))


def _ssm(u, h0_re, h0_im, tables, *, layer, tc):
    nb, t, ssm_w = u.shape
    n_state = tables[0].shape[-1]
    n_parts = ssm_w // LANES
    assert t % tc == 0 and nb % 8 == 0 and tables[2].shape[1:] == (n_parts, LANES, n_state // n_parts)
    state = pl.BlockSpec((nb, n_state), lambda i: (0, 0))
    seq = pl.BlockSpec((nb, tc, ssm_w), lambda i: (0, i, 0))
    return pl.pallas_call(
        functools.partial(_ssm_kernel, nb=nb, tc=tc),
        grid=(t // tc,),
        in_specs=[seq, state, state] + [_of_layer(a, layer) for a in tables],
        out_specs=[seq, state, state],
        out_shape=[jax.ShapeDtypeStruct(u.shape, F32), jax.ShapeDtypeStruct((nb, n_state), F32),
                   jax.ShapeDtypeStruct((nb, n_state), F32)],
        scratch_shapes=[pltpu.VMEM((n_parts, nb * tc, LANES), F32)] * 2
        + [pltpu.VMEM((nb * tc, n_state // n_parts), F32)] * (2 * n_parts),
        compiler_params=pltpu.CompilerParams(dimension_semantics=("arbitrary",),
                                             vmem_limit_bytes=V7X_VMEM_LIMIT),
        name="ssm",
    )(u, h0_re, h0_im, *tables)


def _merge_rows(x_ref, on_ref, z_ref, wgate_ref, wau_ref, wglu_ref, wout_ref, g_ref, b_ref, alpha):
    d = x_ref.shape[1]
    x = x_ref[...]
    xb = x.astype(BF16)
    on = on_ref[...].astype(BF16)
    zb = z_ref[...].astype(BF16)

    def branch_matmuls(c0):
        cols = slice(c0, c0 + MERGE_CHUNK)
        gcols = slice(d + c0, d + c0 + MERGE_CHUNK)
        return (_dot(xb, wgate_ref[:, cols]), _dot(xb, wgate_ref[:, gcols]),
                _dot(on, wau_ref[:, cols]), _dot(zb, wglu_ref[:, cols]), _dot(zb, wglu_ref[:, gcols]))

    def mix_out(c0, parts):
        ga, gb, a_branch, glu_v, glu_g = parts
        b_branch = glu_v * jax.nn.sigmoid(glu_g)
        mix_in = jax.nn.sigmoid(ga) * a_branch + jax.nn.sigmoid(gb) * b_branch
        return _dot(mix_in.astype(BF16), wout_ref[c0:c0 + MERGE_CHUNK, :])

    acc = alpha * x
    pending = None
    for c0 in range(0, d, MERGE_CHUNK):
        parts = branch_matmuls(c0)
        if pending is not None:
            acc = acc + mix_out(*pending)
        pending = (c0, parts)
    acc = acc + mix_out(*pending)
    return _layer_norm(acc, g_ref[...], b_ref[...])


def _merge_kernel(*refs, alpha):
    *in_refs, o_ref = refs
    o_ref[...] = _merge_rows(*in_refs, alpha)


def _merge(x, o_n, z, w_gate, w_att_up, w_glu, w_out, ln_g, ln_b, *, layer, tm, alpha):
    n, d = x.shape
    row = lambda a: pl.BlockSpec((tm, a.shape[1]), lambda i: (i, 0))
    params = (w_gate, w_att_up, w_glu, w_out, ln_g, ln_b)
    return pl.pallas_call(
        functools.partial(_merge_kernel, alpha=alpha),
        grid=(n // tm,),
        in_specs=[row(x), row(o_n), row(z)] + [_of_layer(a, layer) for a in params],
        out_specs=row(x),
        out_shape=jax.ShapeDtypeStruct(x.shape, F32),
        compiler_params=pltpu.CompilerParams(dimension_semantics=("parallel",),
                                             vmem_limit_bytes=V7X_VMEM_LIMIT),
        name="merge",
    )(x, o_n, z, w_gate, w_att_up, w_glu, w_out, ln_g, ln_b)


def _ffn_rows(x, wup_ref, wdn_ref, g_ref, b_ref, alpha, ff_chunk):
    xb = x.astype(BF16)
    acc = alpha * x
    for c0 in range(0, wup_ref.shape[1], ff_chunk):
        hid = jnp.maximum(_dot(xb, wup_ref[:, c0:c0 + ff_chunk]), 0.0)
        acc = acc + _dot((hid * hid).astype(BF16), wdn_ref[c0:c0 + ff_chunk, :])
    return _layer_norm(acc, g_ref[...], b_ref[...])


def _ffn_kernel(x_ref, wup_ref, wdn_ref, g_ref, b_ref, o_ref, *, alpha, ff_chunk):
    o_ref[...] = _ffn_rows(x_ref[...], wup_ref, wdn_ref, g_ref, b_ref, alpha, ff_chunk)


def _merge_ffn_kernel(*refs, alpha, ff_chunk):
    *merge_refs, wup_ref, wdn_ref, g2_ref, b2_ref, o_ref = refs
    x1 = _merge_rows(*merge_refs, alpha)
    o_ref[...] = _ffn_rows(x1, wup_ref, wdn_ref, g2_ref, b2_ref, alpha, ff_chunk)


def _merge_ffn(x, o_n, z, merge_params, ffn_params, *, layer, tm, alpha):
    n, d = x.shape
    row = lambda a: pl.BlockSpec((tm, a.shape[1]), lambda i: (i, 0))
    params = tuple(merge_params) + tuple(ffn_params)
    return pl.pallas_call(
        functools.partial(_merge_ffn_kernel, alpha=alpha, ff_chunk=d),
        grid=(n // tm,),
        in_specs=[row(x), row(o_n), row(z)] + [_of_layer(a, layer) for a in params],
        out_specs=row(x),
        out_shape=jax.ShapeDtypeStruct(x.shape, F32),
        compiler_params=pltpu.CompilerParams(dimension_semantics=("parallel",),
                                             vmem_limit_bytes=V7X_VMEM_LIMIT),
        name="merge_ffn",
    )(x, o_n, z, *params)


def _ffn(x, w_up, w_down, ln_g, ln_b, *, layer, tm, alpha):
    n, d = x.shape
    row = pl.BlockSpec((tm, d), lambda i: (i, 0))
    return pl.pallas_call(
        functools.partial(_ffn_kernel, alpha=alpha, ff_chunk=d),
        grid=(n // tm,),
        in_specs=[row] + [_of_layer(a, layer) for a in (w_up, w_down, ln_g, ln_b)],
        out_specs=row,
        out_shape=jax.ShapeDtypeStruct(x.shape, F32),
        compiler_params=pltpu.CompilerParams(dimension_semantics=("parallel",),
                                             vmem_limit_bytes=V7X_VMEM_LIMIT),
        name="ffn",
    )(x, w_up, w_down, ln_g, ln_b)


def _ffn_skip_meta(x, w_up, w_down, ln_g, ln_b, *, layer, tm, alpha):
    bsz, t, d = x.shape
    seq = t - N_META
    assert seq % tm == 0
    window = pl.BlockSpec((pl.Squeezed(), pl.Element(tm), pl.Element(d)),
                          lambda b, j: (b, pl.multiple_of(N_META + j * tm, N_META), 0))
    return pl.pallas_call(
        functools.partial(_ffn_kernel, alpha=alpha, ff_chunk=d),
        grid=(bsz, seq // tm),
        in_specs=[window] + [_of_layer(a, layer) for a in (w_up, w_down, ln_g, ln_b)],
        out_specs=pl.BlockSpec((None, tm, d), lambda b, j: (b, j, 0)),
        out_shape=jax.ShapeDtypeStruct((bsz, seq, d), F32),
        compiler_params=pltpu.CompilerParams(dimension_semantics=("parallel", "parallel"),
                                             vmem_limit_bytes=V7X_VMEM_LIMIT),
        name="ffn_skip_meta",
    )(x, w_up, w_down, ln_g, ln_b)


def _ssm_tables(a_re, a_im, log_dt, b_re, b_im, c_re, c_im, d):
    depth, n_groups, n_p = a_re.shape
    a_re = a_re.astype(F32)
    a_im = a_im.astype(F32)
    dt = jnp.exp(log_dt.astype(F32))[..., None]
    mag = jnp.exp(a_re * dt)
    lb_re = mag * jnp.cos(a_im * dt)
    lb_im = mag * jnp.sin(a_im * dt)
    nr, ni = lb_re - 1.0, lb_im
    den = a_re * a_re + a_im * a_im
    f_re = (nr * a_re + ni * a_im) / den
    f_im = (ni * a_re - nr * a_im) / den
    b_re = b_re.astype(F32)
    b_im = b_im.astype(F32)
    bb_re = f_re[..., None] * b_re - f_im[..., None] * b_im
    bb_im = f_re[..., None] * b_im + f_im[..., None] * b_re
    gpp = LANES // GROUP_CH
    n_parts = n_groups // gpp
    eye = jnp.eye(gpp, dtype=F32)

    def in_mat(bb):
        bb = bb.reshape(depth, n_parts, gpp, n_p, GROUP_CH)
        full = jnp.einsum('lngpc,gh->lngchp', bb, eye)
        return full.reshape(depth, n_parts, LANES, gpp * n_p).astype(BF16)

    def out_mat(cc):
        cc = cc.reshape(depth, n_parts, gpp, GROUP_CH, n_p)
        full = jnp.einsum('lngcp,gh->lngphc', cc, eye)
        return full.reshape(depth, n_parts, gpp * n_p, LANES).astype(BF16)

    sublanes = lambda a: jnp.broadcast_to(a.reshape(depth, 1, -1), (depth, 8, n_groups * n_p))
    return (sublanes(lb_re), sublanes(lb_im), in_mat(bb_re), in_mat(bb_im),
            out_mat(c_re.astype(F32)), out_mat(-c_im.astype(F32)), d.astype(F32).reshape(depth, 1, -1))


def kernel(x_prompt, x_sample, cache_k, cache_v, state_ssm_re, state_ssm_im, page_table, meta_tokens, ln_in_g, ln_in_b, rel_bias, w_in, lambda_q1, lambda_k1, lambda_q2, lambda_k2, subln_g, w_att_up, ssm_a_re, ssm_a_im, ssm_log_dt, ssm_b_re, ssm_b_im, ssm_c_re, ssm_c_im, ssm_d, w_glu, w_out, ln1_g, ln1_b, w_up, w_down, ln2_g, ln2_b):
    bsz, seq, d_model = x_prompt.shape
    dec_b, dec_seq, _ = x_sample.shape
    depth = w_in.shape[0]
    att_w = w_att_up.shape[1]
    ssm_w = w_glu.shape[1]
    n_heads = att_w // HEAD_WIDTH
    n_groups = ssm_w // GROUP_CH
    n_state = n_groups * STATE_DIM
    t = N_META + seq
    past = page_table.shape[1] * PAGE_SIZE
    alpha = (2.0 * depth) ** 0.25
    assert seq % ATT_Q_TILE == 0 and t % SSM_CHUNK == 0 and past >= 2 * PAGE_SIZE

    tm_p = _row_tile(t, 704)
    tm_s = _row_tile(dec_b * dec_seq, 512)

    xp = x_prompt
    kt_all = v_all = None
    xs = x_sample.reshape(dec_b * dec_seq, d_model)
    row = lambda a: a.astype(F32).reshape(1, -1)
    near, mb, mm, bias_s = _bias_tables(rel_bias, ATT_Q_TILE, dec_seq)
    n_pool = cache_k.shape[1]
    cache_kt = jnp.transpose(cache_k, (0, 1, 3, 4, 5, 2)).reshape(depth, n_pool, att_w, PAGE_SIZE)
    cache_v = cache_v.reshape(depth, n_pool, PAGE_SIZE * n_heads, HEAD_WIDTH)
    h0p = jnp.zeros((bsz, n_state), F32)

    lam_init = [0.8 - 0.6 * math.exp(-0.3 * l) for l in range(depth)]
    f32 = lambda a: a.astype(F32)
    lam = (jnp.exp(jnp.sum(f32(lambda_q1) * f32(lambda_k1), -1))
           - jnp.exp(jnp.sum(f32(lambda_q2) * f32(lambda_k2), -1)) + jnp.asarray(lam_init, F32))
    rows = lambda a: a.astype(F32).reshape(depth, 1, -1)
    g_sub = rows(subln_g)
    n_qkvu = 3 * att_w + ssm_w
    w_qkvu_bf = w_in[:, :, :n_qkvu].astype(BF16)
    w_gate_bf = w_in[:, :, n_qkvu:].astype(BF16)
    merge_params = (w_gate_bf, w_att_up.astype(BF16), w_glu.astype(BF16), w_out.astype(BF16),
                    rows(ln1_g), rows(ln1_b))
    ffn_params = (w_up.astype(BF16), w_down.astype(BF16), rows(ln2_g), rows(ln2_b))
    ssm_tables = _ssm_tables(ssm_a_re, ssm_a_im, ssm_log_dt, ssm_b_re, ssm_b_im, ssm_c_re, ssm_c_im,
                             ssm_d)
    state_re = state_ssm_re.reshape(depth, dec_b, n_state).astype(F32)
    state_im = state_ssm_im.reshape(depth, dec_b, n_state).astype(F32)

    outs = {name: [] for name in ("hpr", "hpi", "ks", "vs", "hsr", "hsi")}
    for l in range(depth):
        sub_scale = 1.0 - lam_init[l]

        *x_ln, q, kf, vb, u, v_all = _in_proj_prompt(
            xp, meta_tokens.astype(F32), row(ln_in_g), row(ln_in_b), w_qkvu_bf, v_all, layer=l,
            depth=depth, bsz=bsz, t=t, tm=tm_p, att_w=att_w, ssm_w=ssm_w)
        xp = x_ln[0] if x_ln else xp
        *x_ln, qs, ks_new, vs_new, us = _in_proj(xs, row(ln_in_g), row(ln_in_b), w_qkvu_bf, layer=l,
                                                 tm=tm_s, att_w=att_w, ssm_w=ssm_w)
        xs = x_ln[0] if x_ln else xs
        seq3 = lambda a: a.reshape(bsz, t, a.shape[-1])
        dec3 = lambda a: a.reshape(dec_b, dec_seq, a.shape[-1])
        o_n, kt_all, os_n = _attn(page_table, lam, seq3(q), seq3(kf), seq3(vb), near, mb, mm, g_sub,
                                  kt_all, dec3(qs), dec3(ks_new), dec3(vs_new), bias_s,
                                  cache_kt, cache_v, layer=l, depth=depth, sub_scale=sub_scale)

        z, hpr, hpi = _ssm(seq3(u), h0p, h0p, ssm_tables, layer=l, tc=SSM_CHUNK)
        xp = _merge(xp, o_n.reshape(bsz * t, att_w), z.reshape(bsz * t, ssm_w), *merge_params,
                    layer=l, tm=tm_p, alpha=alpha)
        if l < depth - 1:
            xp = _ffn(xp, *ffn_params, layer=l, tm=tm_p, alpha=alpha)
        else:
            y_prompt = _ffn_skip_meta(xp.reshape(bsz, t, d_model), *ffn_params, layer=l,
                                      tm=_row_tile(seq, 512), alpha=alpha)
        outs["hpr"].append(hpr.reshape(bsz, n_groups, STATE_DIM))
        outs["hpi"].append(hpi.reshape(bsz, n_groups, STATE_DIM))

        z, hsr, hsi = _ssm(dec3(us), state_re[l], state_im[l], ssm_tables, layer=l, tc=dec_seq)
        xs = _merge_ffn(xs, os_n.reshape(dec_b * dec_seq, att_w), z.reshape(dec_b * dec_seq, ssm_w),
                        merge_params, ffn_params, layer=l, tm=tm_s, alpha=alpha)
        outs["ks"].append(ks_new.reshape(dec_b, dec_seq, n_heads, 2, HEAD_DIM))
        outs["vs"].append(vs_new.reshape(dec_b, dec_seq, n_heads, HEAD_WIDTH))
        outs["hsr"].append(hsr.reshape(dec_b, n_groups, STATE_DIM))
        outs["hsi"].append(hsi.reshape(dec_b, n_groups, STATE_DIM))

    y_sample = xs.reshape(dec_b, dec_seq, d_model)
    st = lambda name: jnp.stack(outs[name])
    k_prompt = jnp.transpose(kt_all.reshape(depth, bsz, n_heads, 2, HEAD_DIM, t), (0, 1, 5, 2, 3, 4))
    v_prompt = v_all.reshape(depth, bsz, t, n_heads, HEAD_WIDTH)
    return (y_prompt, y_sample, k_prompt, v_prompt, st("hpr"), st("hpi"), st("ks"), st("vs"),
            st("hsr"), st("hsi"))
```
